```python
import math
import jax
import jax.numpy as jnp
from jax import lax
import numpy as np

D_MODEL = 1024
BATCH = 16
SEQ = 2048
DEPTH = 4
DEC_BATCH = 128
DEC_SEQ = 8
PAST_LEN = 8192
PAGE_SIZE = 128

N_MIXERS = 3
N_A = (DEPTH + 2) // 3
N_B = (DEPTH + 1) // 3
N_C = DEPTH // 3

MLA_HEADS = 16
MLA_Q_RANK = 384
MLA_KV_RANK = 256
MLA_NOPE = 64
MLA_ROPE = 32
MLA_V = 64
ROPE_THETA = 10000.0
MLA_SCALE = (MLA_NOPE + MLA_ROPE) ** -0.5

DSW_GROUPS = ((128, 1), (512, 4), (2048, 16))
DSW_SPAN = 128
DSW_HEADS = 8
DSW_HEAD_DIM = 64
DSW_SCALE = DSW_HEAD_DIM ** -0.5
REL_BUCKETS = 32
REL_MAX_DIST = 2048

FOX_HEADS = 16
FOX_HEAD_DIM = 64
FOX_SCALE = FOX_HEAD_DIM ** -0.5
FOX_GATE_BIAS = 4.0

FFN_HIDDEN = -(-8 * D_MODEL // (3 * 256)) * 256

Q_BLOCK = 128
RMS_EPS = 1e-6
NEG_INF = -1e30

kernel_name = 'hybrid_mla_dilated_fox_decoder'


def rms_norm(x, g):
    xf = x.astype(jnp.float32)
    y = xf * lax.rsqrt(jnp.mean(xf * xf, axis=-1, keepdims=True) + RMS_EPS)
    return (y * g.astype(jnp.float32)).astype(x.dtype)


def rope(x, pos):
    half = x.shape[-1] // 2
    inv = ROPE_THETA ** (-jnp.arange(half, dtype=jnp.float32) / half)
    ang = pos.astype(jnp.float32)[:, None] * inv
    ang = ang.reshape(ang.shape[:1] + (1,) * (x.ndim - 3) + ang.shape[1:])
    cos, sin = jnp.cos(ang), jnp.sin(ang)
    x1 = x[..., :half].astype(jnp.float32)
    x2 = x[..., half:].astype(jnp.float32)
    return jnp.concatenate([x1 * cos - x2 * sin, x2 * cos + x1 * sin], axis=-1).astype(x.dtype)


def rel_bucket(dist):
    max_exact = REL_BUCKETS // 2
    d = dist.astype(jnp.float32)
    log_b = max_exact + jnp.log(jnp.maximum(d, 1.0) / max_exact) / math.log(REL_MAX_DIST / max_exact) * (REL_BUCKETS - max_exact)
    log_b = jnp.minimum(log_b.astype(jnp.int32), REL_BUCKETS - 1)
    return jnp.where(dist < max_exact, dist, log_b)


def swiglu(h, w_gate, w_up, w_down):
    return (jax.nn.silu(h @ w_gate) * (h @ w_up)) @ w_down


def paged_online_softmax(xs, page_fn, s_new, v_new, weigh):
    m = jnp.max(s_new, axis=-1)
    p = jnp.exp(s_new - m[..., None])
    init = (m, jnp.sum(p, axis=-1), weigh(p, v_new))

    def step(carry, xp):
        m_old, l_old, acc_old = carry
        s, vals = page_fn(xp)
        m_new = jnp.maximum(m_old, jnp.max(s, axis=-1))
        corr = jnp.exp(m_old - m_new)
        pp = jnp.exp(s - m_new[..., None])
        return (m_new, l_old * corr + jnp.sum(pp, axis=-1), acc_old * corr[..., None] + weigh(pp, vals)), None

    (m, l, acc), _ = lax.scan(step, init, xs)
    return acc / l[..., None]


def mla_project(x, pos, w_dq, g_q, w_uq, w_dkv, g_kv):
    b, s_len, _ = x.shape
    q = (rms_norm(x @ w_dq, g_q) @ w_uq).reshape(b, s_len, MLA_HEADS, MLA_NOPE + MLA_ROPE)
    q_nope, q_pe = q[..., :MLA_NOPE], rope(q[..., MLA_NOPE:], pos)
    kv_a = x @ w_dkv
    c_kv = rms_norm(kv_a[..., :MLA_KV_RANK], g_kv)
    k_pe = rope(kv_a[..., MLA_KV_RANK:], pos)
    return q_nope, q_pe, c_kv, k_pe


def mla_prompt(x, w_dq, g_q, w_uq, w_dkv, g_kv, w_ukv, w_o):
    b, s_len, _ = x.shape
    pos = jnp.arange(s_len)
    q_nope, q_pe, c_kv, k_pe = mla_project(x, pos, w_dq, g_q, w_uq, w_dkv, g_kv)
    kv = (c_kv @ w_ukv).reshape(b, s_len, MLA_HEADS, MLA_NOPE + MLA_V)
    k_nope, v = kv[..., :MLA_NOPE], kv[..., MLA_NOPE:]

    def block(i):
        q0 = i * Q_BLOCK
        qn = lax.dynamic_slice_in_dim(q_nope, q0, Q_BLOCK, axis=1)
        qp = lax.dynamic_slice_in_dim(q_pe, q0, Q_BLOCK, axis=1)
        sc = (jnp.einsum('bqhn,bkhn->bhqk', qn, k_nope, preferred_element_type=jnp.float32)
              + jnp.einsum('bqhr,bkr->bhqk', qp, k_pe, preferred_element_type=jnp.float32)) * MLA_SCALE
        causal = (q0 + jnp.arange(Q_BLOCK))[:, None] >= pos[None, :]
        p = jax.nn.softmax(jnp.where(causal, sc, NEG_INF), axis=-1)
        return jnp.einsum('bhqk,bkhv->bqhv', p.astype(v.dtype), v)

    o = lax.map(block, jnp.arange(s_len // Q_BLOCK))
    o = o.transpose(1, 0, 2, 3, 4).reshape(b, s_len, MLA_HEADS * MLA_V)
    return o @ w_o, c_kv, k_pe


def mla_sample(x, cache_ckv, cache_kpe, layer, page_table, w_dq, g_q, w_uq, w_dkv, g_kv, w_ukv, w_o):
    b, n_new, _ = x.shape
    pos = PAST_LEN + jnp.arange(n_new)
    q_nope, q_pe, c_kv, k_pe = mla_project(x, pos, w_dq, g_q, w_uq, w_dkv, g_kv)
    w_ukv_h = w_ukv.reshape(MLA_KV_RANK, MLA_HEADS, MLA_NOPE + MLA_V)
    q_lat = jnp.einsum('bqhn,lhn->bqhl', q_nope, w_ukv_h[..., :MLA_NOPE])

    def scores(ckv, kpe):
        return (jnp.einsum('bqhl,bkl->bhqk', q_lat, ckv, preferred_element_type=jnp.float32)
                + jnp.einsum('bqhr,bkr->bhqk', q_pe, kpe, preferred_element_type=jnp.float32)) * MLA_SCALE

    causal = jnp.arange(n_new)[:, None] >= jnp.arange(n_new)[None, :]
    s_new = jnp.where(causal, scores(c_kv, k_pe), NEG_INF)

    def page_fn(phys):
        ckv = cache_ckv[layer, phys]
        return scores(ckv, cache_kpe[layer, phys]), ckv

    def weigh(p, vals):
        return jnp.einsum('bhqk,bkl->bhql', p, vals, preferred_element_type=jnp.float32)

    o_lat = paged_online_softmax(page_table.T, page_fn, s_new, c_kv, weigh)
    o = jnp.einsum('bhql,lhv->bqhv', o_lat.astype(x.dtype), w_ukv_h[..., MLA_NOPE:])
    return o.reshape(b, n_new, MLA_HEADS * MLA_V) @ w_o, c_kv, k_pe


def dsw_group_prompt(q, k, v, dil, bias_tab):
    b, s_len, h, dh = q.shape
    L = s_len // dil
    qb = math.gcd(L, Q_BLOCK)
    nb = L // qb
    kw = qb + DSW_SPAN

    def to_sub(t):
        return t.reshape(b, L, dil, h, dh).transpose(0, 2, 1, 3, 4).reshape(b * dil, L, h, dh)

    pad = ((0, 0), (DSW_SPAN, 0), (0, 0), (0, 0))
    qs = to_sub(q).reshape(b * dil, nb, qb, h, dh)
    ks = jnp.pad(to_sub(k), pad)
    vs = jnp.pad(to_sub(v), pad)
    idx = jnp.arange(nb)[:, None] * qb + jnp.arange(kw)[None, :]
    kb, vb = ks[:, idx], vs[:, idx]
    step = jnp.arange(qb)[:, None] + DSW_SPAN - jnp.arange(kw)[None, :]
    valid = (step >= 0) & (step <= DSW_SPAN) & (idx >= DSW_SPAN)[:, None, :]
    bias = bias_tab[rel_bucket(jnp.clip(step, 0, DSW_SPAN) * dil)].transpose(2, 0, 1)
    sc = jnp.einsum('znqhd,znkhd->znhqk', qs, kb, preferred_element_type=jnp.float32) * DSW_SCALE + bias
    sc = jnp.where(valid[None, :, None], sc, NEG_INF)
    m = jnp.max(sc, axis=-1)
    p = jnp.exp(sc - m[..., None])
    l = jnp.sum(p, axis=-1)
    o = jnp.einsum('znhqk,znkhd->znqhd', p, vb, preferred_element_type=jnp.float32) / l.transpose(0, 1, 3, 2)[..., None]
    lse = (m + jnp.log(l)).transpose(0, 1, 3, 2)

    def from_sub(t):
        rest = t.shape[3:]
        t = t.reshape((b, dil, L) + rest)
        return t.transpose((0, 2, 1) + tuple(range(3, 3 + len(rest)))).reshape((b, s_len) + rest)

    return from_sub(o), from_sub(lse)


def dsw_group_sample(q, k, v, buf, dil, bias_tab):
    b, n_new, h, dh = q.shape
    wb = buf.shape[1]
    kk = jnp.concatenate([buf[:, :, 0], k], axis=1)
    vv = jnp.concatenate([buf[:, :, 1], v], axis=1)
    steps = jnp.arange(DSW_SPAN + 1)
    idx = wb + jnp.arange(n_new)[:, None] - dil * steps[None, :]
    valid = idx >= 0
    idx = jnp.maximum(idx, 0)
    kg, vg = kk[:, idx], vv[:, idx]
    bias = bias_tab[rel_bucket(steps * dil)].T
    sc = jnp.einsum('bqhd,bqkhd->bhqk', q, kg, preferred_element_type=jnp.float32) * DSW_SCALE + bias[None, :, None, :]
    sc = jnp.where(valid[None, None], sc, NEG_INF)
    m = jnp.max(sc, axis=-1)
    p = jnp.exp(sc - m[..., None])
    l = jnp.sum(p, axis=-1)
    o = jnp.einsum('bhqk,bqkhd->bqhd', p, vg, preferred_element_type=jnp.float32) / l.transpose(0, 2, 1)[..., None]
    lse = (m + jnp.log(l)).transpose(0, 2, 1)
    new_buf = jnp.concatenate([buf, jnp.stack([k, v], axis=2)], axis=1)[:, n_new:]
    return o, lse, new_buf


def dsw_merge(outs, lses, dtype):
    wts = jax.nn.softmax(jnp.stack(lses), axis=0)
    o = jnp.sum(wts[..., None] * jnp.stack(outs), axis=0)
    return o.reshape(o.shape[:2] + (-1,)).astype(dtype)


def dsw_split(x, w_qkv):
    b, s_len, _ = x.shape
    return (x @ w_qkv).reshape(b, s_len, len(DSW_GROUPS), 3, DSW_HEADS, DSW_HEAD_DIM)


def dsw_prompt(x, w_qkv, w_o, rel_bias):
    s_len = x.shape[1]
    qkv = dsw_split(x, w_qkv)
    outs, lses, states = [], [], []
    for g, (win, dil) in enumerate(DSW_GROUPS):
        q, k, v = qkv[:, :, g, 0], qkv[:, :, g, 1], qkv[:, :, g, 2]
        o, lse = dsw_group_prompt(q, k, v, dil, rel_bias[:, g * DSW_HEADS:(g + 1) * DSW_HEADS])
        outs.append(o)
        lses.append(lse)
        states.append(jnp.stack([k, v], axis=2)[:, s_len - min(win, s_len):])
    return dsw_merge(outs, lses, x.dtype) @ w_o, states


def dsw_sample(x, bufs, w_qkv, w_o, rel_bias):
    qkv = dsw_split(x, w_qkv)
    outs, lses, states = [], [], []
    for g, (win, dil) in enumerate(DSW_GROUPS):
        q, k, v = qkv[:, :, g, 0], qkv[:, :, g, 1], qkv[:, :, g, 2]
        o, lse, nbuf = dsw_group_sample(q, k, v, bufs[g], dil, rel_bias[:, g * DSW_HEADS:(g + 1) * DSW_HEADS])
        outs.append(o)
        lses.append(lse)
        states.append(nbuf)
    return dsw_merge(outs, lses, x.dtype) @ w_o, states


def fox_project(x, w_qkv, w_f, b_f):
    b, s_len, _ = x.shape
    qkv = (x @ w_qkv).reshape(b, s_len, 3, FOX_HEADS, FOX_HEAD_DIM)
    logf = jax.nn.log_sigmoid((x @ w_f + b_f).astype(jnp.float32))
    return qkv[:, :, 0], qkv[:, :, 1], qkv[:, :, 2], logf


def fox_prompt(x, w_qkv, w_f, b_f, w_o):
    b, s_len, _ = x.shape
    q, k, v, logf = fox_project(x, w_qkv, w_f, b_f)
    c = jnp.cumsum(logf, axis=1).transpose(0, 2, 1)
    pos = jnp.arange(s_len)

    def block(i):
        q0 = i * Q_BLOCK
        qb = lax.dynamic_slice_in_dim(q, q0, Q_BLOCK, axis=1)
        cq = lax.dynamic_slice_in_dim(c, q0, Q_BLOCK, axis=2)
        sc = jnp.einsum('bqhd,bkhd->bhqk', qb, k, preferred_element_type=jnp.float32) * FOX_SCALE + cq[..., :, None] - c[..., None, :]
        causal = (q0 + jnp.arange(Q_BLOCK))[:, None] >= pos[None, :]
        p = jax.nn.softmax(jnp.where(causal, sc, NEG_INF), axis=-1)
        return jnp.einsum('bhqk,bkhd->bqhd', p.astype(v.dtype), v)

    o = lax.map(block, jnp.arange(s_len // Q_BLOCK))
    o = o.transpose(1, 0, 2, 3, 4).reshape(b, s_len, FOX_HEADS * FOX_HEAD_DIM)
    return o @ w_o, jnp.stack([k, v], axis=2), logf


def fox_sample(x, cache_kv, cache_logf, layer, page_table, w_qkv, w_f, b_f, w_o):
    b, n_new, _ = x.shape
    n_pages = page_table.shape[1]
    q, k, v, logf = fox_project(x, w_qkv, w_f, b_f)
    logf_past = cache_logf[layer, page_table].astype(jnp.float32).reshape(b, n_pages * PAGE_SIZE, FOX_HEADS)
    c_past = jnp.cumsum(logf_past, axis=1)
    c_new = (c_past[:, -1:] + jnp.cumsum(logf, axis=1)).transpose(0, 2, 1)
    c_pages = c_past.reshape(b, n_pages, PAGE_SIZE, FOX_HEADS).transpose(1, 0, 3, 2)

    def scores(keys, c_keys):
        return jnp.einsum('bqhd,bkhd->bhqk', q, keys, preferred_element_type=jnp.float32) * FOX_SCALE + c_new[..., :, None] - c_keys[..., None, :]

    causal = jnp.arange(n_new)[:, None] >= jnp.arange(n_new)[None, :]
    s_new = jnp.where(causal, scores(k, c_new), NEG_INF)

    def page_fn(xp):
        phys, c_keys = xp
        kv = cache_kv[layer, phys]
        return scores(kv[:, :, 0], c_keys), kv[:, :, 1]

    def weigh(p, vals):
        return jnp.einsum('bhqk,bkhd->bhqd', p, vals, preferred_element_type=jnp.float32)

    o = paged_online_softmax((page_table.T, c_pages), page_fn, s_new, v, weigh)
    o = o.transpose(0, 2, 1, 3).reshape(b, n_new, FOX_HEADS * FOX_HEAD_DIM).astype(x.dtype)
    return o @ w_o, jnp.stack([k, v], axis=2), logf


def stack_field(rows, i):
    return jnp.stack([r[i] for r in rows])


def setup_inputs(seed: int = 0) -> dict:
    key = jax.random.key(seed)
    keys = iter(jax.random.split(key, 64))

    def normal(shape, scale=1.0):
        return scale * jax.random.normal(next(keys), shape, jnp.float32)

    def weight(shape):
        return normal(shape, shape[-2] ** -0.5)

    def gain(shape):
        return 1.0 + normal(shape, 0.1)

    n_pages = PAST_LEN // PAGE_SIZE
    n_used = DEC_BATCH * n_pages
    n_pool = n_used + max(1, n_used // 4)
    page_table = jax.random.permutation(next(keys), n_pool)[:n_used].reshape(DEC_BATCH, n_pages).astype(jnp.int32)
    dsw_bufs = [normal((N_B, DEC_BATCH, min(win, PAST_LEN), 2, DSW_HEADS, DSW_HEAD_DIM)) for win, _ in DSW_GROUPS]
    return {
        'x_prompt': normal((BATCH, SEQ, D_MODEL)),
        'x_sample': normal((DEC_BATCH, DEC_SEQ, D_MODEL)),
        'page_table': page_table,
        'cache_mla_ckv': normal((N_A, n_pool, PAGE_SIZE, MLA_KV_RANK)),
        'cache_mla_kpe': normal((N_A, n_pool, PAGE_SIZE, MLA_ROPE)),
        'state_dsw_kv_g0': dsw_bufs[0],
        'state_dsw_kv_g1': dsw_bufs[1],
        'state_dsw_kv_g2': dsw_bufs[2],
        'cache_fox_kv': normal((N_C, n_pool, PAGE_SIZE, 2, FOX_HEADS, FOX_HEAD_DIM)),
        'cache_fox_logf': jax.nn.log_sigmoid(FOX_GATE_BIAS + normal((N_C, n_pool, PAGE_SIZE, FOX_HEADS))),
        'rel_bias': normal((REL_BUCKETS, len(DSW_GROUPS) * DSW_HEADS), 0.2),
        'g_mix': gain((DEPTH, D_MODEL)),
        'g_ffn': gain((DEPTH, D_MODEL)),
        'g_final': gain((D_MODEL,)),
        'w_ffn_gate': weight((DEPTH, D_MODEL, FFN_HIDDEN)),
        'w_ffn_up': weight((DEPTH, D_MODEL, FFN_HIDDEN)),
        'w_ffn_down': weight((DEPTH, FFN_HIDDEN, D_MODEL)),
        'a_w_dq': weight((N_A, D_MODEL, MLA_Q_RANK)),
        'a_g_q': gain((N_A, MLA_Q_RANK)),
        'a_w_uq': weight((N_A, MLA_Q_RANK, MLA_HEADS * (MLA_NOPE + MLA_ROPE))),
        'a_w_dkv': weight((N_A, D_MODEL, MLA_KV_RANK + MLA_ROPE)),
        'a_g_kv': gain((N_A, MLA_KV_RANK)),
        'a_w_ukv': weight((N_A, MLA_KV_RANK, MLA_HEADS * (MLA_NOPE + MLA_V))),
        'a_w_o': weight((N_A, MLA_HEADS * MLA_V, D_MODEL)),
        'b_w_qkv': weight((N_B, D_MODEL, len(DSW_GROUPS) * 3 * DSW_HEADS * DSW_HEAD_DIM)),
        'b_w_o': weight((N_B, DSW_HEADS * DSW_HEAD_DIM, D_MODEL)),
        'c_w_qkv': weight((N_C, D_MODEL, 3 * FOX_HEADS * FOX_HEAD_DIM)),
        'c_w_f': weight((N_C, D_MODEL, FOX_HEADS)),
        'c_b_f': FOX_GATE_BIAS + normal((N_C, FOX_HEADS), 0.5),
        'c_w_o': weight((N_C, FOX_HEADS * FOX_HEAD_DIM, D_MODEL)),
    }


def reference(x_prompt, x_sample, page_table, cache_mla_ckv, cache_mla_kpe, state_dsw_kv_g0, state_dsw_kv_g1, state_dsw_kv_g2, cache_fox_kv, cache_fox_logf, rel_bias, g_mix, g_ffn, g_final, w_ffn_gate, w_ffn_up, w_ffn_down, a_w_dq, a_g_q, a_w_uq, a_w_dkv, a_g_kv, a_w_ukv, a_w_o, b_w_qkv, b_w_o, c_w_qkv, c_w_f, c_b_f, c_w_o):
    xp, xs = x_prompt, x_sample
    mla_p, mla_s, dsw_p, dsw_s, fox_p, fox_s = [], [], [], [], [], []
    for i in range(DEPTH):
        kind, j = i % N_MIXERS, i // N_MIXERS
        hp, hs = rms_norm(xp, g_mix[i]), rms_norm(xs, g_mix[i])
        if kind == 0:
            wa = (a_w_dq[j], a_g_q[j], a_w_uq[j], a_w_dkv[j], a_g_kv[j], a_w_ukv[j], a_w_o[j])
            yp, ckv_p, kpe_p = mla_prompt(hp, *wa)
            ys, ckv_s, kpe_s = mla_sample(hs, cache_mla_ckv, cache_mla_kpe, j, page_table, *wa)
            mla_p.append((ckv_p, kpe_p))
            mla_s.append((ckv_s, kpe_s))
        elif kind == 1:
            yp, st_p = dsw_prompt(hp, b_w_qkv[j], b_w_o[j], rel_bias)
            ys, st_s = dsw_sample(hs, (state_dsw_kv_g0[j], state_dsw_kv_g1[j], state_dsw_kv_g2[j]), b_w_qkv[j], b_w_o[j], rel_bias)
            dsw_p.append(st_p)
            dsw_s.append(st_s)
        else:
            wc = (c_w_qkv[j], c_w_f[j], c_b_f[j], c_w_o[j])
            yp, kv_p, lf_p = fox_prompt(hp, *wc)
            ys, kv_s, lf_s = fox_sample(hs, cache_fox_kv, cache_fox_logf, j, page_table, *wc)
            fox_p.append((kv_p, lf_p))
            fox_s.append((kv_s, lf_s))
        xp = xp + yp
        xs = xs + ys
        wf = (w_ffn_gate[i], w_ffn_up[i], w_ffn_down[i])
        xp = xp + swiglu(rms_norm(xp, g_ffn[i]), *wf)
        xs = xs + swiglu(rms_norm(xs, g_ffn[i]), *wf)
    y_prompt = rms_norm(xp, g_final)
    y_sample = rms_norm(xs, g_final)
    return (y_prompt, y_sample,
            stack_field(mla_p, 0), stack_field(mla_p, 1), stack_field(mla_s, 0), stack_field(mla_s, 1),
            stack_field(dsw_p, 0), stack_field(dsw_p, 1), stack_field(dsw_p, 2),
            stack_field(dsw_s, 0), stack_field(dsw_s, 1), stack_field(dsw_s, 2),
            stack_field(fox_p, 0), stack_field(fox_p, 1), stack_field(fox_s, 0), stack_field(fox_s, 1))
```

```python
import functools
import math

import jax
import jax.numpy as jnp
from jax import lax
from jax.experimental import pallas as pl
from jax.experimental.pallas import tpu as pltpu

F32 = jnp.float32
BF16 = jnp.bfloat16

D_MODEL = 1024
PAGE = 128
MLA_HEADS = 16
MLA_Q_RANK = 384
MLA_KV_RANK = 256
MLA_NOPE = 64
MLA_ROPE = 32
MLA_V = 64
ROPE_THETA = 10000.0
MLA_SCALE = (MLA_NOPE + MLA_ROPE) ** -0.5
DSW_GROUPS = ((128, 1), (512, 4), (2048, 16))
DSW_SPAN = 128
DSW_HEADS = 8
DSW_HEAD_DIM = 64
DSW_SCALE = DSW_HEAD_DIM ** -0.5
REL_BUCKETS = 32
REL_MAX_DIST = 2048
FOX_HEADS = 16
FOX_HEAD_DIM = 64
FOX_SCALE = FOX_HEAD_DIM ** -0.5
RMS_EPS = 1e-6
NEG_INF = -1e30

LANES = 128
SUBLANES = 8
HALF = LANES // 2
VMEM_LIMIT = 52 * 1024 * 1024

NT_DIMS = (((1,), (1,)), ((), ()))


def _params(*sem):
    return pltpu.CompilerParams(dimension_semantics=sem, vmem_limit_bytes=VMEM_LIMIT)


def _row_tile(n, pref):
    t = pref
    while t > SUBLANES and n % t:
        t //= 2
    assert n % t == 0, (n, pref)
    return t


def _dot(a, b):
    return jnp.dot(a, b, preferred_element_type=F32)


def _dot_nt(a, b):
    return lax.dot_general(a, b, NT_DIMS, preferred_element_type=F32)


def _rms(x, g):
    var = jnp.mean(x * x, axis=-1, keepdims=True)
    return x * lax.rsqrt(var + RMS_EPS) * g


def _split3(x):
    hi = x.astype(BF16)
    r1 = x - hi.astype(F32)
    mid = r1.astype(BF16)
    lo = (r1 - mid.astype(F32)).astype(BF16)
    return hi, mid, lo


def _pair_select(lane, a, b):
    return jnp.where(lane < HALF, a, b)


def _ffn_kernel(x_ref, g_ref, wg_ref, wu_ref, wd_ref, o_ref, h_sc, acc_sc):
    j = pl.program_id(1)

    @pl.when(j == 0)
    def _():
        x = x_ref[...]
        h_sc[...] = _rms(x, g_ref[...]).astype(BF16)
        acc_sc[...] = x

    h = h_sc[...]
    a = _dot(h, wg_ref[...])
    u = _dot(h, wu_ref[...])
    act = (a * jax.nn.sigmoid(a) * u).astype(BF16)
    acc_sc[...] += _dot(act, wd_ref[...])

    @pl.when(j == pl.num_programs(1) - 1)
    def _():
        o_ref[...] = acc_sc[...]


def ffn_block(x, g, wg, wu, wd):
    n, d = x.shape
    f = wg.shape[1]
    tm = _row_tile(n, 512)
    nf = 2 if f % (2 * LANES) == 0 else 1
    tf = f // nf
    return pl.pallas_call(
        _ffn_kernel,
        grid=(n // tm, nf),
        in_specs=[
            pl.BlockSpec((tm, d), lambda i, j: (i, 0)),
            pl.BlockSpec((1, d), lambda i, j: (0, 0)),
            pl.BlockSpec((d, tf), lambda i, j: (0, j)),
            pl.BlockSpec((d, tf), lambda i, j: (0, j)),
            pl.BlockSpec((tf, d), lambda i, j: (j, 0)),
        ],
        out_specs=pl.BlockSpec((tm, d), lambda i, j: (i, 0)),
        out_shape=jax.ShapeDtypeStruct((n, d), F32),
        scratch_shapes=[pltpu.VMEM((tm, d), BF16), pltpu.VMEM((tm, d), F32)],
        compiler_params=_params("parallel", "arbitrary"),
        name="ffn_block",
    )(x, g.reshape(1, d), wg, wu, wd)


def _matmul_res_kernel(a_ref, w_ref, r_ref, o_ref):
    o_ref[...] = r_ref[...] + _dot(a_ref[...].astype(BF16), w_ref[...])


def matmul_residual(a, w, res):
    n, k = a.shape
    d = w.shape[1]
    tm = _row_tile(n, 512)
    return pl.pallas_call(
        _matmul_res_kernel,
        grid=(n // tm,),
        in_specs=[
            pl.BlockSpec((tm, k), lambda i: (i, 0)),
            pl.BlockSpec((k, d), lambda i: (0, 0)),
            pl.BlockSpec((tm, d), lambda i: (i, 0)),
        ],
        out_specs=pl.BlockSpec((tm, d), lambda i: (i, 0)),
        out_shape=jax.ShapeDtypeStruct((n, d), F32),
        compiler_params=_params("parallel"),
        name="matmul_residual",
    )(a, w, res)


def _final_norm_kernel(x_ref, g_ref, o_ref):
    o_ref[...] = _rms(x_ref[...], g_ref[...])


def final_norm(x, g):
    n, d = x.shape
    tm = _row_tile(n, 1024)
    return pl.pallas_call(
        _final_norm_kernel,
        grid=(n // tm,),
        in_specs=[pl.BlockSpec((tm, d), lambda i: (i, 0)), pl.BlockSpec((1, d), lambda i: (0, 0))],
        out_specs=pl.BlockSpec((tm, d), lambda i: (i, 0)),
        out_shape=jax.ShapeDtypeStruct((n, d), F32),
        compiler_params=_params("parallel"),
        name="final_norm",
    )(x, g.reshape(1, d))


ROPE_LO = MLA_NOPE
ROPE_HALF = MLA_ROPE // 2


def _rope_tables(pos):
    inv = ROPE_THETA ** (-jnp.arange(ROPE_HALF, dtype=F32) / ROPE_HALF)
    ang = pos.astype(F32)[:, None] * inv
    cos, sin = jnp.cos(ang), jnp.sin(ang)
    n = pos.shape[0]
    ones = jnp.ones((n, ROPE_LO), F32)
    zl = jnp.zeros((n, ROPE_LO), F32)
    zh = jnp.zeros((n, ROPE_HALF), F32)
    zt = jnp.zeros((n, LANES - ROPE_LO - MLA_ROPE), F32)
    c = jnp.concatenate([ones, cos, cos, ones[:, : zt.shape[1]]], axis=1)
    s_up = jnp.concatenate([zl, zh, sin, zt], axis=1)
    s_dn = jnp.concatenate([zl, -sin, zh, zt], axis=1)
    return c, s_up, s_dn


def _rope_tile(t, cos, s_up, s_dn):
    return (t * cos + pltpu.roll(t, ROPE_HALF, 1) * s_up
            + pltpu.roll(t, LANES - ROPE_HALF, 1) * s_dn)


def _mla_proj_kernel(x_ref, gm_ref, wdq_ref, gq_ref, wuq_ref, wdc_ref, wdr_ref, gkv_ref,
                     wuk_ref, wuv_ref, cos_ref, sup_ref, sdn_ref,
                     q_ref, kf_ref, v_ref, ckv_ref, kpe_ref):
    cos, s_up, s_dn = cos_ref[...], sup_ref[...], sdn_ref[...]
    h = _rms(x_ref[...], gm_ref[...]).astype(BF16)
    qn = _rms(_dot(h, wdq_ref[...]), gq_ref[...]).astype(BF16)
    q = _dot(qn, wuq_ref[...])
    for hh in range(MLA_HEADS):
        sl = slice(hh * LANES, (hh + 1) * LANES)
        q_ref[:, sl] = (_rope_tile(q[:, sl], cos, s_up, s_dn) * MLA_SCALE).astype(BF16)
    ckv = _rms(_dot(h, wdc_ref[...]), gkv_ref[...])
    ckv_ref[...] = ckv
    kpe = _rope_tile(_dot(h, wdr_ref[...]), cos, s_up, s_dn)
    kpe_ref[...] = kpe
    cb = ckv.astype(BF16)
    kn = _dot(cb, wuk_ref[...])
    for hh in range(MLA_HEADS):
        sl = slice(hh * LANES, (hh + 1) * LANES)
        kf_ref[:, sl] = (kn[:, sl] + kpe).astype(BF16)
    v_ref[...] = _dot(cb, wuv_ref[...]).astype(BF16)


def mla_project(x, g_mix, w, tables, tab_blocks):
    n, d = x.shape
    tm = _row_tile(n, 256)
    tb = tab_blocks // tm if tab_blocks >= tm else 1
    hq = MLA_HEADS * LANES
    full = lambda a: pl.BlockSpec(a.shape, lambda i: (0,) * a.ndim)
    tab = pl.BlockSpec((tm, LANES), lambda i: (i % tb, 0))
    row = lambda wd: pl.BlockSpec((tm, wd), lambda i: (i, 0))
    ws = (w["wdq"], w["gq"], w["wuq"], w["wdc"], w["wdr"], w["gkv"], w["wuk"], w["wuv"])
    return pl.pallas_call(
        _mla_proj_kernel,
        grid=(n // tm,),
        in_specs=[row(d), full(g_mix)] + [full(a) for a in ws] + [tab, tab, tab],
        out_specs=[row(hq), row(hq), row(MLA_HEADS * MLA_V), row(MLA_KV_RANK), row(LANES)],
        out_shape=[
            jax.ShapeDtypeStruct((n, hq), BF16),
            jax.ShapeDtypeStruct((n, hq), BF16),
            jax.ShapeDtypeStruct((n, MLA_HEADS * MLA_V), BF16),
            jax.ShapeDtypeStruct((n, MLA_KV_RANK), F32),
            jax.ShapeDtypeStruct((n, LANES), F32),
        ],
        compiler_params=_params("parallel"),
        name="mla_project",
    )(x, g_mix, *ws, *tables)


def _flash_kernel(*refs, tq, fox):
    if fox:
        q_ref, k_ref, v_ref, ccol_ref, crow_ref, o_ref = refs
    else:
        q_ref, k_ref, v_ref, o_ref = refs
    qi = pl.program_id(2)
    lane = lax.broadcasted_iota(jnp.int32, (tq, LANES), 1)
    row = lax.broadcasted_iota(jnp.int32, (tq, tq), 0)
    col = lax.broadcasted_iota(jnp.int32, (tq, tq), 1)
    outs = []
    for hh in range(2):
        if fox:
            qf = q_ref[...].astype(F32)
            keep = (lane < HALF) if hh == 0 else (lane >= HALF)
            qh = jnp.where(keep, qf, 0.0).astype(BF16)
            kcol = slice(0, LANES)
        else:
            qh = q_ref[:, hh * LANES:(hh + 1) * LANES]
            kcol = slice(hh * LANES, (hh + 1) * LANES)

        def step(j, carry, masked):
            m, l, acc = carry
            koff = pl.multiple_of(j * tq, tq)
            s = _dot_nt(qh, k_ref[pl.ds(koff, tq), kcol])
            if fox:
                s = s + (ccol_ref[0, 0, :, hh:hh + 1] - crow_ref[0, 0, hh:hh + 1, pl.ds(koff, tq)])
            if masked:
                s = jnp.where(col <= row, s, NEG_INF)
            m_new = jnp.maximum(m, jnp.max(s, axis=-1, keepdims=True))
            corr = jnp.exp(m - m_new)
            p = jnp.exp(s - m_new)
            l = l * corr + jnp.sum(p, axis=-1, keepdims=True)
            acc = acc * corr + _dot(p.astype(BF16), v_ref[pl.ds(koff, tq), :])
            return m_new, l, acc

        init = (jnp.full((tq, 1), NEG_INF, F32), jnp.zeros((tq, 1), F32), jnp.zeros((tq, LANES), F32))
        carry = lax.fori_loop(0, qi, lambda j, c: step(j, c, False), init)
        _, l, acc = step(qi, carry, True)
        outs.append(acc / l)
    o_ref[...] = _pair_select(lane, outs[0], outs[1]).astype(BF16)


def flash_prompt(q, k, v, batch, seq, *, fox, ccol=None, crow=None):
    n = batch * seq
    pairs = v.shape[1] // LANES
    wq = q.shape[1] // pairs
    tq = _row_tile(seq, 256)
    nq = seq // tq
    in_specs = [
        pl.BlockSpec((tq, wq), lambda b, p, i: (b * nq + i, p)),
        pl.BlockSpec((seq, wq), lambda b, p, i: (b, p)),
        pl.BlockSpec((seq, LANES), lambda b, p, i: (b, p)),
    ]
    args = [q, k, v]
    if fox:
        in_specs += [
            pl.BlockSpec((1, 1, tq, 2), lambda b, p, i: (b, p, i, 0)),
            pl.BlockSpec((1, 1, 2, seq), lambda b, p, i: (b, p, 0, 0)),
        ]
        args += [ccol, crow]
    return pl.pallas_call(
        functools.partial(_flash_kernel, tq=tq, fox=fox),
        grid=(batch, pairs, nq),
        in_specs=in_specs,
        out_specs=pl.BlockSpec((tq, LANES), lambda b, p, i: (b * nq + i, p)),
        out_shape=jax.ShapeDtypeStruct((n, pairs * LANES), BF16),
        compiler_params=_params("parallel", "parallel", "arbitrary"),
        name="flash_fox" if fox else "flash_mla",
    )(*args)


QW = MLA_KV_RANK + LANES


def _absorb_kernel(q_ref, w_ref, o_ref):
    r = _dot(q_ref[...], w_ref[0])
    o_ref[...] = r.reshape(o_ref.shape)


def mla_absorb(q, w_abs, bd, nq):
    n = q.shape[0]
    return pl.pallas_call(
        _absorb_kernel,
        grid=(MLA_HEADS,),
        in_specs=[pl.BlockSpec((n, LANES), lambda h: (0, h)),
                  pl.BlockSpec((1, LANES, QW), lambda h: (h, 0, 0))],
        out_specs=pl.BlockSpec((bd, 1, nq, QW), lambda h: (0, h, 0, 0)),
        out_shape=jax.ShapeDtypeStruct((bd, MLA_HEADS, nq, QW), F32),
        compiler_params=_params("parallel"),
        name="mla_absorb",
    )(q, w_abs)


def _online_update(s, vals, m_sc, l_sc, acc_sc):
    m_old = m_sc[...]
    m_new = jnp.maximum(m_old, jnp.max(s, axis=-1, keepdims=True))
    corr = jnp.exp(m_old - m_new)
    p = jnp.exp(s - m_new)
    l_sc[...] = l_sc[...] * corr + jnp.sum(p, axis=-1, keepdims=True)
    acc_sc[...] = acc_sc[...] * corr + _dot(p.astype(BF16), vals)
    m_sc[...] = m_new


def _softmax_init(m_sc, l_sc, acc_sc):
    m_sc[...] = jnp.full(m_sc.shape, NEG_INF, F32)
    l_sc[...] = jnp.zeros(l_sc.shape, F32)
    acc_sc[...] = jnp.zeros(acc_sc.shape, F32)


def _new_key_mask(rows, nq):
    r = lax.broadcasted_iota(jnp.int32, (rows, PAGE), 0)
    c = lax.broadcasted_iota(jnp.int32, (rows, PAGE), 1)
    return c <= (r % nq)


def _mla_sample_kernel(pt_ref, q_ref, *rest, pp, nq):
    ckv_refs = rest[:pp]
    kpe_refs = rest[pp:2 * pp]
    cn_ref, pn_ref, o_ref, m_sc, l_sc, acc_sc, kn_sc, rn_sc = rest[2 * pp:]
    j = pl.program_id(1)
    q = q_ref[0]
    ql = q[:, :MLA_KV_RANK].astype(BF16)
    qp = q[:, MLA_KV_RANK:MLA_KV_RANK + MLA_ROPE].astype(BF16)
    rows = q.shape[0]

    def scores(kc, kp):
        return _dot_nt(ql, kc) + _dot_nt(qp, kp)

    @pl.when(j == 0)
    def _():
        _softmax_init(m_sc, l_sc, acc_sc)
        kn_sc[...] = jnp.zeros(kn_sc.shape, F32)
        rn_sc[...] = jnp.zeros(rn_sc.shape, F32)
        kn_sc[0:nq, :] = cn_ref[0]
        rn_sc[0:nq, :] = pn_ref[0]
        kc = kn_sc[...].astype(BF16)
        s = jnp.where(_new_key_mask(rows, nq), scores(kc, rn_sc[...].astype(BF16)), NEG_INF)
        _online_update(s, kc, m_sc, l_sc, acc_sc)

    for t in range(pp):
        kc = ckv_refs[t][0, 0].astype(BF16)
        _online_update(scores(kc, kpe_refs[t][0, 0].astype(BF16)), kc, m_sc, l_sc, acc_sc)

    @pl.when(j == pl.num_programs(1) - 1)
    def _():
        o_ref[0] = acc_sc[...] / l_sc[...]


def mla_sample_attention(qabs, cache_ckv, cache_kpe, layer, page_table, ckv_new, kpe_new, pp):
    bd, rows, _ = qabs.shape
    n_pages = page_table.shape[1]
    nq = ckv_new.shape[1]

    def page_spec(width, t):
        return pl.BlockSpec((1, 1, PAGE, width),
                            lambda b, j, pt: (layer, pt[b, j * pp + t], 0, 0))

    in_specs = ([pl.BlockSpec((1, rows, QW), lambda b, j, pt: (b, 0, 0))]
                + [page_spec(MLA_KV_RANK, t) for t in range(pp)]
                + [page_spec(MLA_ROPE, t) for t in range(pp)]
                + [pl.BlockSpec((1, nq, MLA_KV_RANK), lambda b, j, pt: (b, 0, 0)),
                   pl.BlockSpec((1, nq, MLA_ROPE), lambda b, j, pt: (b, 0, 0))])
    grid_spec = pltpu.PrefetchScalarGridSpec(
        num_scalar_prefetch=1,
        grid=(bd, n_pages // pp),
        in_specs=in_specs,
        out_specs=pl.BlockSpec((1, rows, MLA_KV_RANK), lambda b, j, pt: (b, 0, 0)),
        scratch_shapes=[pltpu.VMEM((rows, 1), F32), pltpu.VMEM((rows, 1), F32),
                        pltpu.VMEM((rows, MLA_KV_RANK), F32),
                        pltpu.VMEM((PAGE, MLA_KV_RANK), F32), pltpu.VMEM((PAGE, MLA_ROPE), F32)],
    )
    return pl.pallas_call(
        functools.partial(_mla_sample_kernel, pp=pp, nq=nq),
        grid_spec=grid_spec,
        out_shape=jax.ShapeDtypeStruct((bd, rows, MLA_KV_RANK), F32),
        compiler_params=_params("parallel", "arbitrary"),
        name="mla_sample_attention",
    )(page_table, qabs, *([cache_ckv] * pp), *([cache_kpe] * pp), ckv_new, kpe_new)


def _unabsorb_kernel(ol_ref, w_ref, o_ref):
    n = o_ref.shape[0]
    a = ol_ref[:, 0].reshape(n, MLA_KV_RANK).astype(BF16)
    b = ol_ref[:, 1].reshape(n, MLA_KV_RANK).astype(BF16)
    o_ref[...] = (_dot(a, w_ref[0, 0]) + _dot(b, w_ref[0, 1])).astype(BF16)


def mla_unabsorb(o_lat, w_unabs, bd, nq):
    n = bd * nq
    pairs = MLA_HEADS // 2
    return pl.pallas_call(
        _unabsorb_kernel,
        grid=(pairs,),
        in_specs=[pl.BlockSpec((bd, 2, nq, MLA_KV_RANK), lambda p: (0, p, 0, 0)),
                  pl.BlockSpec((1, 2, MLA_KV_RANK, LANES), lambda p: (p, 0, 0, 0))],
        out_specs=pl.BlockSpec((n, LANES), lambda p: (0, p)),
        out_shape=jax.ShapeDtypeStruct((n, pairs * LANES), BF16),
        compiler_params=_params("parallel"),
        name="mla_unabsorb",
    )(o_lat, w_unabs)


GW = DSW_HEADS * DSW_HEAD_DIM
N_GROUPS = len(DSW_GROUPS)


def _dsw_proj_kernel(x_ref, g_ref, w_ref, qkv_ref, kv_ref):
    h = _rms(x_ref[...], g_ref[...]).astype(BF16)
    for g in range(N_GROUPS):
        r = _dot(h, w_ref[:, g * 3 * GW:(g + 1) * 3 * GW])
        qkv_ref[:, g * 3 * GW:(g + 1) * 3 * GW] = r.astype(BF16)
        kv_ref[:, g * 2 * GW:(g + 1) * 2 * GW] = r[:, GW:]


def dsw_project(x, g_mix, w_qkv):
    n, d = x.shape
    tm = _row_tile(n, 256)
    wq = N_GROUPS * 3 * GW
    return pl.pallas_call(
        _dsw_proj_kernel,
        grid=(n // tm,),
        in_specs=[pl.BlockSpec((tm, d), lambda i: (i, 0)),
                  pl.BlockSpec((1, d), lambda i: (0, 0)),
                  pl.BlockSpec((d, wq), lambda i: (0, 0))],
        out_specs=[pl.BlockSpec((tm, wq), lambda i: (i, 0)),
                   pl.BlockSpec((tm, N_GROUPS * 2 * GW), lambda i: (i, 0))],
        out_shape=[jax.ShapeDtypeStruct((n, wq), BF16),
                   jax.ShapeDtypeStruct((n, N_GROUPS * 2 * GW), F32)],
        compiler_params=_params("parallel"),
        name="dsw_project",
    )(x, g_mix, w_qkv)


def _rel_bucket(dist):
    max_exact = REL_BUCKETS // 2
    d = dist.astype(F32)
    log_b = max_exact + jnp.log(jnp.maximum(d, 1.0) / max_exact) / math.log(REL_MAX_DIST / max_exact) * (REL_BUCKETS - max_exact)
    log_b = jnp.minimum(log_b.astype(jnp.int32), REL_BUCKETS - 1)
    return jnp.where(dist < max_exact, dist, log_b)


def _dsw_prompt_kernel(q_ref, kc_ref, kp_ref, vc_ref, vp_ref, bias_ref, o_ref, lse_ref, *, has_prev):
    n = pl.program_id(2)
    qb = q_ref.shape[0]
    lane = lax.broadcasted_iota(jnp.int32, (qb, LANES), 1)
    for pr in range(DSW_HEADS // 2):
        sl = slice(pr * LANES, (pr + 1) * LANES)
        qf = q_ref[:, sl].astype(F32)
        kc, vc = kc_ref[:, sl], vc_ref[:, sl]
        o_pair, lse_pair = [], []
        for hh in range(2):
            head = 2 * pr + hh
            keep = (lane < HALF) if hh == 0 else (lane >= HALF)
            qh = jnp.where(keep, qf, 0.0).astype(BF16)
            sc = _dot_nt(qh, kc) + bias_ref[head, :, qb:]
            m = jnp.max(sc, axis=-1, keepdims=True)
            if has_prev:
                sp = _dot_nt(qh, kp_ref[:, sl]) + bias_ref[head, :, :qb]
                sp = jnp.where(n > 0, sp, NEG_INF)
                m = jnp.maximum(m, jnp.max(sp, axis=-1, keepdims=True))
            pc = jnp.exp(sc - m)
            l = jnp.sum(pc, axis=-1, keepdims=True)
            o = _dot(pc.astype(BF16), vc)
            if has_prev:
                pp_ = jnp.exp(sp - m)
                l = l + jnp.sum(pp_, axis=-1, keepdims=True)
                o = o + _dot(pp_.astype(BF16), vp_ref[:, sl])
            o_pair.append(o / l)
            lse_pair.append(jnp.broadcast_to(m + jnp.log(l), (qb, LANES)))
        o_ref[:, sl] = _pair_select(lane, o_pair[0], o_pair[1])
        lse_ref[:, sl] = _pair_select(lane, lse_pair[0], lse_pair[1])


def dsw_prompt_group(qkv, g, dil, bias, batch, seq):
    n = batch * seq
    sub = seq // dil
    qb = DSW_SPAN
    assert sub % qb == 0
    nb = sub // qb
    wq = N_GROUPS * 3 * GW
    blocks_per_row = wq // GW
    view = qkv.reshape(batch * sub, dil * wq)

    def col(kind):
        return lambda b, r, i: (b * nb + i, r * blocks_per_row + g * 3 + kind)

    def col_prev(kind):
        return lambda b, r, i: (b * nb + jnp.maximum(i - 1, 0), r * blocks_per_row + g * 3 + kind)

    blk = lambda f: pl.BlockSpec((qb, GW), f)
    out_blk = pl.BlockSpec((qb, GW), lambda b, r, i: (b * nb + i, r))
    o, lse = pl.pallas_call(
        functools.partial(_dsw_prompt_kernel, has_prev=nb > 1),
        grid=(batch, dil, nb),
        in_specs=[blk(col(0)), blk(col(1)), blk(col_prev(1)), blk(col(2)), blk(col_prev(2)),
                  pl.BlockSpec(bias.shape, lambda b, r, i: (0, 0, 0))],
        out_specs=[out_blk, out_blk],
        out_shape=[jax.ShapeDtypeStruct((batch * sub, dil * GW), F32)] * 2,
        compiler_params=_params("parallel", "parallel", "arbitrary"),
        name=f"dsw_prompt_g{g}",
    )(view, view, view, view, view, bias)
    return o.reshape(n, GW), lse.reshape(n, GW)


def _dsw_prompt_bias(rel_bias, g, dil):
    qb = DSW_SPAN
    step = jnp.arange(qb)[:, None] + DSW_SPAN - jnp.arange(2 * qb)[None, :]
    valid = (step >= 0) & (step <= DSW_SPAN)
    tab = rel_bias[:, g * DSW_HEADS:(g + 1) * DSW_HEADS]
    bias = tab[_rel_bucket(jnp.clip(step, 0, DSW_SPAN) * dil)].transpose(2, 0, 1)
    return jnp.where(valid[None], bias, NEG_INF).astype(F32)


def _dsw_merge_kernel(o0, o1, o2, l0, l1, l2, w_ref, r_ref, out_ref):
    a0, a1, a2 = l0[...], l1[...], l2[...]
    m = jnp.maximum(jnp.maximum(a0, a1), a2)
    e0, e1, e2 = jnp.exp(a0 - m), jnp.exp(a1 - m), jnp.exp(a2 - m)
    merged = (e0 * o0[...] + e1 * o1[...] + e2 * o2[...]) / (e0 + e1 + e2)
    out_ref[...] = r_ref[...] + _dot(merged.astype(BF16), w_ref[...])


def dsw_merge_out(outs, lses, w_o, res):
    n, d = res.shape
    tm = _row_tile(n, 512)
    row = lambda wd: pl.BlockSpec((tm, wd), lambda i: (i, 0))
    return pl.pallas_call(
        _dsw_merge_kernel,
        grid=(n // tm,),
        in_specs=[row(GW)] * 6 + [pl.BlockSpec((GW, d), lambda i: (0, 0)), row(d)],
        out_specs=row(d),
        out_shape=jax.ShapeDtypeStruct((n, d), F32),
        compiler_params=_params("parallel"),
        name="dsw_merge_out",
    )(*outs, *lses, w_o, res)


def _diag_pairs(x, lane, nq, store):
    heads = x.shape[0] // nq
    for pr in range(heads // 2):
        a = x[2 * pr * nq:(2 * pr + 1) * nq, pr * LANES:(pr + 1) * LANES]
        b = x[(2 * pr + 1) * nq:(2 * pr + 2) * nq, pr * LANES:(pr + 1) * LANES]
        store(pr, _pair_select(lane, a, b))


def _dsw_sample_kernel(a_ref, tail_ref, new_ref, q_ref, bias_ref, biasn_ref,
                       nb_ref, o_ref, lse_ref, m_sc, l_sc, acc_sc, kn_sc, *, rblk, nq):
    i = pl.program_id(1)
    last = pl.num_programs(1) - 1
    nb_ref[0, 0:rblk - nq, :] = a_ref[0, nq:rblk, :]
    nb_ref[0, rblk - nq:rblk, :] = jnp.where(i == last, new_ref[0], tail_ref[0])

    q = q_ref[0]

    def update(kv, bias):
        s = _dot_nt(q, kv[:, :GW].astype(BF16)) + bias
        _online_update(s, kv[:, GW:].astype(BF16), m_sc, l_sc, acc_sc)

    @pl.when(i == 0)
    def _():
        _softmax_init(m_sc, l_sc, acc_sc)

    update(a_ref[0], bias_ref[...])

    @pl.when(i == last)
    def _():
        kn_sc[...] = jnp.zeros(kn_sc.shape, F32)
        kn_sc[0:nq, :] = new_ref[0]
        update(kn_sc[...], biasn_ref[...])
        l = l_sc[...]
        o_bd = acc_sc[...] / l
        lse_bd = jnp.broadcast_to(m_sc[...] + jnp.log(l), o_bd.shape)
        lane = lax.broadcasted_iota(jnp.int32, (nq, LANES), 1)

        def store_o(pr, val):
            o_ref[0, :, pr * LANES:(pr + 1) * LANES] = val

        def store_l(pr, val):
            lse_ref[0, :, pr * LANES:(pr + 1) * LANES] = val

        _diag_pairs(o_bd, lane, nq, store_o)
        _diag_pairs(lse_bd, lane, nq, store_l)


def dsw_sample_group(buf, kv_new, q_bd, bias, bias_new):
    bd, wb, roww = buf.shape
    nq = kv_new.shape[1]
    rows = q_bd.shape[1]
    rblk = min(wb, 512)
    nblk = wb // rblk
    tail_blocks = rblk // nq
    n_tail = wb // nq
    out_small = pl.BlockSpec((1, nq, GW), lambda b, i: (b, 0, 0))
    return pl.pallas_call(
        functools.partial(_dsw_sample_kernel, rblk=rblk, nq=nq),
        grid=(bd, nblk),
        in_specs=[
            pl.BlockSpec((1, rblk, roww), lambda b, i: (b, i, 0)),
            pl.BlockSpec((1, nq, roww), lambda b, i: (b, jnp.minimum((i + 1) * tail_blocks, n_tail - 1), 0)),
            pl.BlockSpec((1, nq, roww), lambda b, i: (b, 0, 0)),
            pl.BlockSpec((1, rows, GW), lambda b, i: (b, 0, 0)),
            pl.BlockSpec((rows, rblk), lambda b, i: (0, i)),
            pl.BlockSpec((rows, PAGE), lambda b, i: (0, 0)),
        ],
        out_specs=[pl.BlockSpec((1, rblk, roww), lambda b, i: (b, i, 0)), out_small, out_small],
        out_shape=[jax.ShapeDtypeStruct((bd, wb, roww), F32),
                   jax.ShapeDtypeStruct((bd, nq, GW), F32),
                   jax.ShapeDtypeStruct((bd, nq, GW), F32)],
        scratch_shapes=[pltpu.VMEM((rows, 1), F32), pltpu.VMEM((rows, 1), F32),
                        pltpu.VMEM((rows, GW), F32), pltpu.VMEM((PAGE, roww), F32)],
        compiler_params=_params("parallel", "arbitrary"),
        name=f"dsw_sample_w{wb}",
    )(buf, buf, kv_new, q_bd, bias, bias_new)


def _dsw_sample_bias(rel_bias, g, dil, wb, nq):
    tab = rel_bias[:, g * DSW_HEADS:(g + 1) * DSW_HEADS]
    qi = jnp.arange(nq)[:, None]

    def table(key_idx):
        delta = wb + qi - key_idx[None, :]
        step = delta // dil
        valid = (delta >= 0) & (delta % dil == 0) & (step <= DSW_SPAN)
        b = tab[_rel_bucket(jnp.clip(step, 0, DSW_SPAN) * dil)]
        b = jnp.where(valid[..., None], b, NEG_INF).transpose(2, 0, 1)
        return b.reshape(DSW_HEADS * nq, -1).astype(F32)

    return table(jnp.arange(wb)), table(wb + jnp.arange(PAGE))


def _block_diag_q(q, bd, nq, heads, dh):
    q4 = q.reshape(bd, nq, heads, dh)
    eye = jnp.eye(heads, dtype=q.dtype)
    return jnp.einsum("bqhd,hg->bhqgd", q4, eye).reshape(bd, heads * nq, heads * dh)


FW = FOX_HEADS * FOX_HEAD_DIM


def _log_sigmoid(z):
    return jnp.minimum(z, 0.0) - jnp.log1p(jnp.exp(-jnp.abs(z)))


def _fox_proj_kernel(x_ref, g_ref, w_ref, wf_ref, bf_ref, q_ref, kvb_ref, kv_ref, lf_ref):
    h = _rms(x_ref[...], g_ref[...]).astype(BF16)
    q_ref[...] = _dot(h, w_ref[:, :FW]).astype(BF16)
    kv = _dot(h, w_ref[:, FW:])
    kv_ref[...] = kv
    kvb_ref[...] = kv.astype(BF16)
    lf_ref[...] = _log_sigmoid(_dot(h, wf_ref[...]) + bf_ref[...])


def fox_project(x, g_mix, w_qkv, w_f, b_f):
    n, d = x.shape
    tm = _row_tile(n, 256)
    row = lambda wd: pl.BlockSpec((tm, wd), lambda i: (i, 0))
    full = lambda a: pl.BlockSpec(a.shape, lambda i: (0,) * a.ndim)
    return pl.pallas_call(
        _fox_proj_kernel,
        grid=(n // tm,),
        in_specs=[row(d), full(g_mix), full(w_qkv), full(w_f), full(b_f)],
        out_specs=[row(FW), row(2 * FW), row(2 * FW), row(LANES)],
        out_shape=[jax.ShapeDtypeStruct((n, FW), BF16),
                   jax.ShapeDtypeStruct((n, 2 * FW), BF16),
                   jax.ShapeDtypeStruct((n, 2 * FW), F32),
                   jax.ShapeDtypeStruct((n, LANES), F32)],
        compiler_params=_params("parallel"),
        name="fox_project",
    )(x, g_mix, w_qkv, w_f, b_f)


def _cumsum_kernel(lf_ref, tri_ref, c_ref):
    tri = tri_ref[...]
    carry = jnp.zeros((1, LANES), F32)
    for c in range(lf_ref.shape[1] // PAGE):
        hi, mid, lo = _split3(lf_ref[0, c * PAGE:(c + 1) * PAGE, :])
        y = _dot(tri, hi) + _dot(tri, mid) + _dot(tri, lo) + carry
        c_ref[0, c * PAGE:(c + 1) * PAGE, :] = y
        carry = y[PAGE - 1:PAGE, :]


def cumsum_rows(lf, batch, seq):
    tri = jnp.tril(jnp.ones((PAGE, PAGE), F32)).astype(BF16)
    return pl.pallas_call(
        _cumsum_kernel,
        grid=(batch,),
        in_specs=[pl.BlockSpec((1, seq, LANES), lambda b: (b, 0, 0)),
                  pl.BlockSpec((PAGE, PAGE), lambda b: (0, 0))],
        out_specs=pl.BlockSpec((1, seq, LANES), lambda b: (b, 0, 0)),
        out_shape=jax.ShapeDtypeStruct((batch, seq, LANES), F32),
        compiler_params=_params("parallel"),
        name="cumsum_rows",
    )(lf.reshape(batch, seq, LANES), tri)


def _fox_cum_kernel(pt_ref, *refs, pp):
    page_refs = refs[:pp]
    lfn_ref, triu_ref, ck_ref, cn_ref, carry_sc = refs[pp:]
    j = pl.program_id(1)
    triu = triu_ref[...]

    def lane_cumsum(x):
        hi, mid, lo = _split3(x)
        return _dot(hi, triu) + _dot(mid, triu) + _dot(lo, triu)

    @pl.when(j == 0)
    def _():
        carry_sc[...] = jnp.zeros(carry_sc.shape, F32)

    x = jnp.concatenate([r[0, 0] for r in page_refs], axis=0)
    w = lane_cumsum(x)
    carry = carry_sc[...]
    for t in range(pp):
        wt = w[t * FOX_HEADS:(t + 1) * FOX_HEADS, :]
        ck_ref[0, t] = wt + carry
        carry = carry + wt[:, PAGE - 1:PAGE]
    carry_sc[...] = carry

    @pl.when(j == pl.num_programs(1) - 1)
    def _():
        cn_ref[0] = lane_cumsum(lfn_ref[0]) + carry


def fox_past_cumsum(logf_t, layer, page_table, lf_new_t, pp):
    bd, n_pages = page_table.shape
    triu = jnp.triu(jnp.ones((PAGE, PAGE), F32)).astype(BF16)
    in_specs = ([pl.BlockSpec((1, 1, FOX_HEADS, PAGE), functools.partial(
                    lambda b, j, pt, t: (layer, pt[b, j * pp + t], 0, 0), t=t)) for t in range(pp)]
                + [pl.BlockSpec((1, FOX_HEADS, PAGE), lambda b, j, pt: (b, 0, 0)),
                   pl.BlockSpec((PAGE, PAGE), lambda b, j, pt: (0, 0))])
    grid_spec = pltpu.PrefetchScalarGridSpec(
        num_scalar_prefetch=1,
        grid=(bd, n_pages // pp),
        in_specs=in_specs,
        out_specs=[pl.BlockSpec((1, pp, FOX_HEADS, PAGE), lambda b, j, pt: (b, j, 0, 0)),
                   pl.BlockSpec((1, FOX_HEADS, PAGE), lambda b, j, pt: (b, 0, 0))],
        scratch_shapes=[pltpu.VMEM((FOX_HEADS, 1), F32)],
    )
    return pl.pallas_call(
        functools.partial(_fox_cum_kernel, pp=pp),
        grid_spec=grid_spec,
        out_shape=[jax.ShapeDtypeStruct((bd, n_pages, FOX_HEADS, PAGE), F32),
                   jax.ShapeDtypeStruct((bd, FOX_HEADS, PAGE), F32)],
        compiler_params=_params("parallel", "arbitrary"),
        name="fox_past_cumsum",
    )(page_table, *([logf_t] * pp), lf_new_t, triu)


def _expand_rows(c, nq):
    h = c.shape[0]
    return jnp.broadcast_to(c[:, None, :], (h, nq, c.shape[1])).reshape(h * nq, c.shape[1])


def _fox_sample_kernel(pt_ref, q_ref, *rest, pp, nq):
    kv_refs = rest[:pp]
    ck_ref, ccol_ref, cnt_ref, new_ref, o_ref, m_sc, l_sc, acc_sc, kn_sc = rest[pp:]
    j = pl.program_id(1)
    q = q_ref[0]
    rows = q.shape[0]
    ccol = ccol_ref[0]

    def update(kv, c_keys, mask=None):
        s = _dot_nt(q, kv[:, :FW].astype(BF16)) + (ccol - _expand_rows(c_keys, nq))
        if mask is not None:
            s = jnp.where(mask, s, NEG_INF)
        _online_update(s, kv[:, FW:].astype(BF16), m_sc, l_sc, acc_sc)

    @pl.when(j == 0)
    def _():
        _softmax_init(m_sc, l_sc, acc_sc)
        kn_sc[...] = jnp.zeros(kn_sc.shape, F32)
        kn_sc[0:nq, :] = new_ref[0]
        update(kn_sc[...], cnt_ref[0], _new_key_mask(rows, nq))

    for t in range(pp):
        update(kv_refs[t][0, 0], ck_ref[0, t])

    @pl.when(j == pl.num_programs(1) - 1)
    def _():
        o_bd = acc_sc[...] / l_sc[...]
        lane = lax.broadcasted_iota(jnp.int32, (nq, LANES), 1)

        def store(pr, val):
            o_ref[0, :, pr * LANES:(pr + 1) * LANES] = val

        _diag_pairs(o_bd, lane, nq, store)


def fox_sample_attention(q_bd, cache_kv, layer, page_table, c_keys, c_col, c_new_t, kv_new, pp):
    bd, rows, _ = q_bd.shape
    n_pages = page_table.shape[1]
    nq = kv_new.shape[1]
    in_specs = ([pl.BlockSpec((1, rows, FW), lambda b, j, pt: (b, 0, 0))]
                + [pl.BlockSpec((1, 1, PAGE, 2 * FW), functools.partial(
                    lambda b, j, pt, t: (layer, pt[b, j * pp + t], 0, 0), t=t)) for t in range(pp)]
                + [pl.BlockSpec((1, pp, FOX_HEADS, PAGE), lambda b, j, pt: (b, j, 0, 0)),
                   pl.BlockSpec((1, rows, 1), lambda b, j, pt: (b, 0, 0)),
                   pl.BlockSpec((1, FOX_HEADS, PAGE), lambda b, j, pt: (b, 0, 0)),
                   pl.BlockSpec((1, nq, 2 * FW), lambda b, j, pt: (b, 0, 0))])
    grid_spec = pltpu.PrefetchScalarGridSpec(
        num_scalar_prefetch=1,
        grid=(bd, n_pages // pp),
        in_specs=in_specs,
        out_specs=pl.BlockSpec((1, nq, FW), lambda b, j, pt: (b, 0, 0)),
        scratch_shapes=[pltpu.VMEM((rows, 1), F32), pltpu.VMEM((rows, 1), F32),
                        pltpu.VMEM((rows, FW), F32), pltpu.VMEM((PAGE, 2 * FW), F32)],
    )
    return pl.pallas_call(
        functools.partial(_fox_sample_kernel, pp=pp, nq=nq),
        grid_spec=grid_spec,
        out_shape=jax.ShapeDtypeStruct((bd, nq, FW), F32),
        compiler_params=_params("parallel", "arbitrary"),
        name="fox_sample_attention",
    )(page_table, q_bd, *([cache_kv] * pp), c_keys, c_col, c_new_t, kv_new)


def _mla_weights(w_dq, g_q, w_uq, w_dkv, g_kv, w_ukv, w_o):
    pad = LANES - MLA_NOPE - MLA_ROPE
    uq = w_uq.reshape(MLA_Q_RANK, MLA_HEADS, MLA_NOPE + MLA_ROPE)
    uq = jnp.pad(uq, ((0, 0), (0, 0), (0, pad))).reshape(MLA_Q_RANK, MLA_HEADS * LANES)
    ukv = w_ukv.reshape(MLA_KV_RANK, MLA_HEADS, MLA_NOPE + MLA_V)
    uk, uv = ukv[..., :MLA_NOPE], ukv[..., MLA_NOPE:]
    uk_pad = jnp.pad(uk, ((0, 0), (0, 0), (0, LANES - MLA_NOPE))).reshape(MLA_KV_RANK, MLA_HEADS * LANES)
    wdr = jnp.pad(w_dkv[:, MLA_KV_RANK:], ((0, 0), (ROPE_LO, pad)))
    w_abs = jnp.zeros((MLA_HEADS, LANES, QW), F32)
    w_abs = w_abs.at[:, :MLA_NOPE, :MLA_KV_RANK].set(uk.transpose(1, 2, 0))
    w_abs = w_abs.at[:, ROPE_LO:ROPE_LO + MLA_ROPE, MLA_KV_RANK:MLA_KV_RANK + MLA_ROPE].set(
        jnp.broadcast_to(jnp.eye(MLA_ROPE, dtype=F32), (MLA_HEADS, MLA_ROPE, MLA_ROPE)))
    uvh = uv.transpose(1, 0, 2).reshape(MLA_HEADS // 2, 2, MLA_KV_RANK, MLA_V)
    w_un = jnp.zeros((MLA_HEADS // 2, 2, MLA_KV_RANK, LANES), F32)
    w_un = w_un.at[:, 0, :, :MLA_V].set(uvh[:, 0]).at[:, 1, :, MLA_V:].set(uvh[:, 1])
    return {
        "wdq": w_dq.astype(BF16), "gq": g_q.reshape(1, -1), "wuq": uq.astype(BF16),
        "wdc": w_dkv[:, :MLA_KV_RANK].astype(BF16), "wdr": wdr.astype(BF16), "gkv": g_kv.reshape(1, -1),
        "wuk": uk_pad.astype(BF16), "wuv": uv.reshape(MLA_KV_RANK, MLA_HEADS * MLA_V).astype(BF16),
        "wabs": w_abs.astype(BF16), "wun": w_un.astype(BF16), "wo": w_o.astype(BF16),
    }


def kernel(x_prompt, x_sample, page_table, cache_mla_ckv, cache_mla_kpe, state_dsw_kv_g0, state_dsw_kv_g1, state_dsw_kv_g2, cache_fox_kv, cache_fox_logf, rel_bias, g_mix, g_ffn, g_final, w_ffn_gate, w_ffn_up, w_ffn_down, a_w_dq, a_g_q, a_w_uq, a_w_dkv, a_g_kv, a_w_ukv, a_w_o, b_w_qkv, b_w_o, c_w_qkv, c_w_f, c_b_f, c_w_o):
    batch, seq, d = x_prompt.shape
    bd, nq, _ = x_sample.shape
    n_pages = page_table.shape[1]
    past = n_pages * PAGE
    depth = g_mix.shape[0]
    n_p, n_s = batch * seq, bd * nq
    page_table = page_table.astype(jnp.int32)
    dsw_states = (state_dsw_kv_g0, state_dsw_kv_g1, state_dsw_kv_g2)

    xp = x_prompt.reshape(n_p, d)
    xs = x_sample.reshape(n_s, d)

    rope_p = _rope_tables(jnp.arange(seq))
    rope_s = _rope_tables(jnp.tile(past + jnp.arange(nq), bd))

    fox_logf_t = cache_fox_logf.transpose(0, 1, 3, 2)
    fox_kv_pages = cache_fox_kv.reshape(cache_fox_kv.shape[:3] + (2 * FW,))

    mla_p, mla_s, dsw_p, dsw_s, fox_p, fox_s = [], [], [], [], [], []
    for i in range(depth):
        kind, j = i % 3, i // 3
        gm = g_mix[i].reshape(1, d)
        if kind == 0:
            w = _mla_weights(a_w_dq[j], a_g_q[j], a_w_uq[j], a_w_dkv[j], a_g_kv[j], a_w_ukv[j], a_w_o[j])
            q, kf, v, ckv, kpe = mla_project(xp, gm, w, rope_p, seq)
            o = flash_prompt(q, kf, v, batch, seq, fox=False)
            xp = matmul_residual(o, w["wo"], xp)
            mla_p.append((ckv.reshape(batch, seq, MLA_KV_RANK),
                          kpe[:, ROPE_LO:ROPE_LO + MLA_ROPE].reshape(batch, seq, MLA_ROPE)))
            q, _, _, ckv, kpe = mla_project(xs, gm, w, rope_s, n_s)
            ckv_new = ckv.reshape(bd, nq, MLA_KV_RANK)
            kpe_new = kpe[:, ROPE_LO:ROPE_LO + MLA_ROPE].reshape(bd, nq, MLA_ROPE)
            qabs = mla_absorb(q, w["wabs"], bd, nq).reshape(bd, MLA_HEADS * nq, QW)
            o_lat = mla_sample_attention(qabs, cache_mla_ckv, cache_mla_kpe, j, page_table,
                                         ckv_new, kpe_new, pp=8 if n_pages % 8 == 0 else 1)
            o = mla_unabsorb(o_lat.reshape(bd, MLA_HEADS, nq, MLA_KV_RANK), w["wun"], bd, nq)
            xs = matmul_residual(o, w["wo"], xs)
            mla_s.append((ckv_new, kpe_new))
        elif kind == 1:
            qscale = jnp.tile(jnp.concatenate([jnp.full((GW,), DSW_SCALE, F32), jnp.ones((2 * GW,), F32)]), N_GROUPS)
            w_qkv = (b_w_qkv[j] * qscale[None, :]).astype(BF16)
            w_o = b_w_o[j].astype(BF16)
            qkv, kv32 = dsw_project(xp, gm, w_qkv)
            outs, lses, st = [], [], []
            for g, (win, dil) in enumerate(DSW_GROUPS):
                o_g, lse_g = dsw_prompt_group(qkv, g, dil, _dsw_prompt_bias(rel_bias, g, dil), batch, seq)
                outs.append(o_g)
                lses.append(lse_g)
                kv_g = kv32[:, g * 2 * GW:(g + 1) * 2 * GW].reshape(batch, seq, 2, DSW_HEADS, DSW_HEAD_DIM)
                st.append(kv_g[:, seq - min(win, seq):])
            xp = dsw_merge_out(outs, lses, w_o, xp)
            dsw_p.append(st)
            qkv, kv32 = dsw_project(xs, gm, w_qkv)
            outs, lses, st = [], [], []
            for g, (win, dil) in enumerate(DSW_GROUPS):
                buf = dsw_states[g][j]
                wb = buf.shape[1]
                q_bd = _block_diag_q(qkv[:, g * 3 * GW:g * 3 * GW + GW], bd, nq, DSW_HEADS, DSW_HEAD_DIM)
                kv_new = kv32[:, g * 2 * GW:(g + 1) * 2 * GW].reshape(bd, nq, 2 * GW)
                bias, bias_new = _dsw_sample_bias(rel_bias, g, dil, wb, nq)
                nbuf, o_g, lse_g = dsw_sample_group(buf.reshape(bd, wb, 2 * GW), kv_new, q_bd, bias, bias_new)
                outs.append(o_g.reshape(n_s, GW))
                lses.append(lse_g.reshape(n_s, GW))
                st.append(nbuf.reshape(buf.shape))
            xs = dsw_merge_out(outs, lses, w_o, xs)
            dsw_s.append(st)
        else:
            qscale = jnp.concatenate([jnp.full((FW,), FOX_SCALE, F32), jnp.ones((2 * FW,), F32)])
            w_qkv = (c_w_qkv[j] * qscale[None, :]).astype(BF16)
            w_f = jnp.pad(c_w_f[j], ((0, 0), (0, LANES - FOX_HEADS))).astype(BF16)
            b_f = jnp.pad(c_b_f[j], (0, LANES - FOX_HEADS)).reshape(1, LANES)
            w_o = c_w_o[j].astype(BF16)
            q, kvb, kv32, lf = fox_project(xp, gm, w_qkv, w_f, b_f)
            c = cumsum_rows(lf, batch, seq)[:, :, :FOX_HEADS]
            ccol = c.reshape(batch, seq, FOX_HEADS // 2, 2).transpose(0, 2, 1, 3)
            crow = c.transpose(0, 2, 1).reshape(batch, FOX_HEADS // 2, 2, seq)
            o = flash_prompt(q, kvb[:, :FW], kvb[:, FW:], batch, seq, fox=True, ccol=ccol, crow=crow)
            xp = matmul_residual(o, w_o, xp)
            fox_p.append((kv32.reshape(batch, seq, 2, FOX_HEADS, FOX_HEAD_DIM),
                          lf[:, :FOX_HEADS].reshape(batch, seq, FOX_HEADS)))
            q, _, kv32, lf = fox_project(xs, gm, w_qkv, w_f, b_f)
            lf_new = lf[:, :FOX_HEADS].reshape(bd, nq, FOX_HEADS)
            lf_new_t = jnp.pad(lf_new.transpose(0, 2, 1), ((0, 0), (0, 0), (0, PAGE - nq)))
            c_keys, c_new_t = fox_past_cumsum(fox_logf_t, j, page_table, lf_new_t,
                                              pp=16 if n_pages % 16 == 0 else 1)
            c_col = c_new_t[:, :, :nq].reshape(bd, FOX_HEADS * nq, 1)
            q_bd = _block_diag_q(q, bd, nq, FOX_HEADS, FOX_HEAD_DIM)
            o = fox_sample_attention(q_bd, fox_kv_pages, j, page_table, c_keys, c_col, c_new_t,
                                     kv32.reshape(bd, nq, 2 * FW), pp=2 if n_pages % 2 == 0 else 1)
            xs = matmul_residual(o.reshape(n_s, FW), w_o, xs)
            fox_s.append((kv32.reshape(bd, nq, 2, FOX_HEADS, FOX_HEAD_DIM), lf_new))
        wg, wu, wd = w_ffn_gate[i].astype(BF16), w_ffn_up[i].astype(BF16), w_ffn_down[i].astype(BF16)
        xp = ffn_block(xp, g_ffn[i], wg, wu, wd)
        xs = ffn_block(xs, g_ffn[i], wg, wu, wd)

    y_prompt = final_norm(xp, g_final).reshape(batch, seq, d)
    y_sample = final_norm(xs, g_final).reshape(bd, nq, d)
    stack = lambda rows, k: jnp.stack([r[k] for r in rows])
    return (y_prompt, y_sample,
            stack(mla_p, 0), stack(mla_p, 1), stack(mla_s, 0), stack(mla_s, 1),
            stack(dsw_p, 0), stack(dsw_p, 1), stack(dsw_p, 2),
            stack(dsw_s, 0), stack(dsw_s, 1), stack(dsw_s, 2),
            stack(fox_p, 0), stack(fox_p, 1), stack(fox_s, 0), stack(fox_s, 1))
```

```python
import functools
import math

import jax
import jax.numpy as jnp
from jax import lax
from jax.experimental import pallas as pl
from jax.experimental.pallas import tpu as pltpu

F32 = jnp.float32
BF16 = jnp.bfloat16

D_MODEL = 1024
PAGE = 128
MLA_HEADS = 16
MLA_Q_RANK = 384
MLA_KV_RANK = 256
MLA_NOPE = 64
MLA_ROPE = 32
MLA_V = 64
ROPE_THETA = 10000.0
MLA_SCALE = (MLA_NOPE + MLA_ROPE) ** -0.5
DSW_GROUPS = ((128, 1), (512, 4), (2048, 16))
DSW_SPAN = 128
DSW_HEADS = 8
DSW_HEAD_DIM = 64
DSW_SCALE = DSW_HEAD_DIM ** -0.5
REL_BUCKETS = 32
REL_MAX_DIST = 2048
FOX_HEADS = 16
FOX_HEAD_DIM = 64
FOX_SCALE = FOX_HEAD_DIM ** -0.5
RMS_EPS = 1e-6
NEG_INF = -1e30

LANES = 128
SUBLANES = 8
HALF = LANES // 2
VMEM_LIMIT = 52 * 1024 * 1024

NT_DIMS = (((1,), (1,)), ((), ()))


def _params(*sem):
    return pltpu.CompilerParams(dimension_semantics=sem, vmem_limit_bytes=VMEM_LIMIT)


def _row_tile(n, pref):
    t = pref
    while t > SUBLANES and n % t:
        t //= 2
    assert n % t == 0, (n, pref)
    return t


def _dot(a, b):
    return jnp.dot(a, b, preferred_element_type=F32)


def _dot_nt(a, b):
    return lax.dot_general(a, b, NT_DIMS, preferred_element_type=F32)


def _rms(x, g):
    var = jnp.mean(x * x, axis=-1, keepdims=True)
    return x * lax.rsqrt(var + RMS_EPS) * g


def _split3(x):
    hi = x.astype(BF16)
    r1 = x - hi.astype(F32)
    mid = r1.astype(BF16)
    lo = (r1 - mid.astype(F32)).astype(BF16)
    return hi, mid, lo


def _pair_select(lane, a, b):
    return jnp.where(lane < HALF, a, b)


def _ffn_kernel(x_ref, g_ref, wg_ref, wu_ref, wd_ref, o_ref, h_sc, acc_sc):
    j = pl.program_id(1)

    @pl.when(j == 0)
    def _():
        x = x_ref[...]
        h_sc[...] = _rms(x, g_ref[...]).astype(BF16)
        acc_sc[...] = x

    h = h_sc[...]
    a = _dot(h, wg_ref[...])
    u = _dot(h, wu_ref[...])
    act = (a * jax.nn.sigmoid(a) * u).astype(BF16)
    acc_sc[...] += _dot(act, wd_ref[...])

    @pl.when(j == pl.num_programs(1) - 1)
    def _():
        o_ref[...] = acc_sc[...]


def ffn_block(x, g, wg, wu, wd):
    n, d = x.shape
    f = wg.shape[1]
    tm = _row_tile(n, 512)
    nf = 2 if f % (2 * LANES) == 0 else 1
    tf = f // nf
    return pl.pallas_call(
        _ffn_kernel,
        grid=(n // tm, nf),
        in_specs=[
            pl.BlockSpec((tm, d), lambda i, j: (i, 0)),
            pl.BlockSpec((1, d), lambda i, j: (0, 0)),
            pl.BlockSpec((d, tf), lambda i, j: (0, j)),
            pl.BlockSpec((d, tf), lambda i, j: (0, j)),
            pl.BlockSpec((tf, d), lambda i, j: (j, 0)),
        ],
        out_specs=pl.BlockSpec((tm, d), lambda i, j: (i, 0)),
        out_shape=jax.ShapeDtypeStruct((n, d), F32),
        scratch_shapes=[pltpu.VMEM((tm, d), BF16), pltpu.VMEM((tm, d), F32)],
        compiler_params=_params("parallel", "arbitrary"),
        name="ffn_block",
    )(x, g.reshape(1, d), wg, wu, wd)


def _matmul_res_kernel(a_ref, w_ref, r_ref, o_ref):
    o_ref[...] = r_ref[...] + _dot(a_ref[...].astype(BF16), w_ref[...])


def matmul_residual(a, w, res):
    n, k = a.shape
    d = w.shape[1]
    tm = _row_tile(n, 512)
    return pl.pallas_call(
        _matmul_res_kernel,
        grid=(n // tm,),
        in_specs=[
            pl.BlockSpec((tm, k), lambda i: (i, 0)),
            pl.BlockSpec((k, d), lambda i: (0, 0)),
            pl.BlockSpec((tm, d), lambda i: (i, 0)),
        ],
        out_specs=pl.BlockSpec((tm, d), lambda i: (i, 0)),
        out_shape=jax.ShapeDtypeStruct((n, d), F32),
        compiler_params=_params("parallel"),
        name="matmul_residual",
    )(a, w, res)


def _final_norm_kernel(x_ref, g_ref, o_ref):
    o_ref[...] = _rms(x_ref[...], g_ref[...])


def final_norm(x, g):
    n, d = x.shape
    tm = _row_tile(n, 1024)
    return pl.pallas_call(
        _final_norm_kernel,
        grid=(n // tm,),
        in_specs=[pl.BlockSpec((tm, d), lambda i: (i, 0)), pl.BlockSpec((1, d), lambda i: (0, 0))],
        out_specs=pl.BlockSpec((tm, d), lambda i: (i, 0)),
        out_shape=jax.ShapeDtypeStruct((n, d), F32),
        compiler_params=_params("parallel"),
        name="final_norm",
    )(x, g.reshape(1, d))


ROPE_LO = MLA_NOPE
ROPE_HALF = MLA_ROPE // 2


def _rope_tables(pos):
    inv = ROPE_THETA ** (-jnp.arange(ROPE_HALF, dtype=F32) / ROPE_HALF)
    ang = pos.astype(F32)[:, None] * inv
    cos, sin = jnp.cos(ang), jnp.sin(ang)
    n = pos.shape[0]
    ones = jnp.ones((n, ROPE_LO), F32)
    zl = jnp.zeros((n, ROPE_LO), F32)
    zh = jnp.zeros((n, ROPE_HALF), F32)
    zt = jnp.zeros((n, LANES - ROPE_LO - MLA_ROPE), F32)
    c = jnp.concatenate([ones, cos, cos, ones[:, : zt.shape[1]]], axis=1)
    s_up = jnp.concatenate([zl, zh, sin, zt], axis=1)
    s_dn = jnp.concatenate([zl, -sin, zh, zt], axis=1)
    return c, s_up, s_dn


def _rope_tile(t, cos, s_up, s_dn):
    return (t * cos + pltpu.roll(t, ROPE_HALF, 1) * s_up
            + pltpu.roll(t, LANES - ROPE_HALF, 1) * s_dn)


def _mla_proj_kernel(x_ref, gm_ref, wdq_ref, gq_ref, wuq_ref, wdc_ref, wdr_ref, gkv_ref,
                     wuk_ref, wuv_ref, cos_ref, sup_ref, sdn_ref,
                     q_ref, kf_ref, v_ref, ckv_ref, kpe_ref):
    cos, s_up, s_dn = cos_ref[...], sup_ref[...], sdn_ref[...]
    h = _rms(x_ref[...], gm_ref[...]).astype(BF16)
    qn = _rms(_dot(h, wdq_ref[...]), gq_ref[...]).astype(BF16)
    q = _dot(qn, wuq_ref[...])
    for hh in range(MLA_HEADS):
        sl = slice(hh * LANES, (hh + 1) * LANES)
        q_ref[:, sl] = (_rope_tile(q[:, sl], cos, s_up, s_dn) * MLA_SCALE).astype(BF16)
    ckv = _rms(_dot(h, wdc_ref[...]), gkv_ref[...])
    ckv_ref[...] = ckv
    kpe = _rope_tile(_dot(h, wdr_ref[...]), cos, s_up, s_dn)
    kpe_ref[...] = kpe
    cb = ckv.astype(BF16)
    kn = _dot(cb, wuk_ref[...])
    for hh in range(MLA_HEADS):
        sl = slice(hh * LANES, (hh + 1) * LANES)
        kf_ref[:, sl] = (kn[:, sl] + kpe).astype(BF16)
    v_ref[...] = _dot(cb, wuv_ref[...]).astype(BF16)


def mla_project(x, g_mix, w, tables, tab_blocks):
    n, d = x.shape
    tm = _row_tile(n, 256)
    tb = tab_blocks // tm if tab_blocks >= tm else 1
    hq = MLA_HEADS * LANES
    full = lambda a: pl.BlockSpec(a.shape, lambda i: (0,) * a.ndim)
    tab = pl.BlockSpec((tm, LANES), lambda i: (i % tb, 0))
    row = lambda wd: pl.BlockSpec((tm, wd), lambda i: (i, 0))
    ws = (w["wdq"], w["gq"], w["wuq"], w["wdc"], w["wdr"], w["gkv"], w["wuk"], w["wuv"])
    return pl.pallas_call(
        _mla_proj_kernel,
        grid=(n // tm,),
        in_specs=[row(d), full(g_mix)] + [full(a) for a in ws] + [tab, tab, tab],
        out_specs=[row(hq), row(hq), row(MLA_HEADS * MLA_V), row(MLA_KV_RANK), row(LANES)],
        out_shape=[
            jax.ShapeDtypeStruct((n, hq), BF16),
            jax.ShapeDtypeStruct((n, hq), BF16),
            jax.ShapeDtypeStruct((n, MLA_HEADS * MLA_V), BF16),
            jax.ShapeDtypeStruct((n, MLA_KV_RANK), F32),
            jax.ShapeDtypeStruct((n, LANES), F32),
        ],
        compiler_params=_params("parallel"),
        name="mla_project",
    )(x, g_mix, *ws, *tables)


def _flash_kernel(*refs, tq, tk, fox):
    if fox:
        q_ref, k_ref, v_ref, ccol_ref, crow_ref, o_ref, s_sc = refs
    else:
        q_ref, k_ref, v_ref, o_ref, s_sc = refs
    qi = pl.program_id(2)
    nblk = (qi * tq) // tk + 1
    ntile = tk // LANES
    lane = lax.broadcasted_iota(jnp.int32, (tq, LANES), 1)
    row = qi * tq + lax.broadcasted_iota(jnp.int32, (tq, tk), 0)
    col = lax.broadcasted_iota(jnp.int32, (tq, tk), 1)
    if fox:
        qf = q_ref[...].astype(F32)
        qh = [jnp.where(lane < HALF, qf, 0.0).astype(BF16), jnp.where(lane >= HALF, qf, 0.0).astype(BF16)]
        kcols = [slice(0, LANES)] * 2
    else:
        qh = [q_ref[:, :LANES], q_ref[:, LANES:2 * LANES]]
        kcols = [slice(0, LANES), slice(LANES, 2 * LANES)]

    def scores(j, mx, masked):
        koff = pl.multiple_of(j * tk, tk)
        out = []
        for hh in range(2):
            s = _dot_nt(qh[hh], k_ref[pl.ds(koff, tk), kcols[hh]])
            if fox:
                s = s + (ccol_ref[0, 0, :, hh:hh + 1] - crow_ref[0, 0, hh:hh + 1, pl.ds(koff, tk)])
            if masked:
                s = jnp.where(col + koff <= row, s, NEG_INF)
            s_sc[hh, :, pl.ds(koff, tk)] = s
            m = mx[hh]
            for c in range(ntile):
                m = jnp.maximum(m, s[:, c * LANES:(c + 1) * LANES])
            out.append(m)
        return tuple(out)

    neg = jnp.full((tq, LANES), NEG_INF, F32)
    mx = lax.fori_loop(0, nblk - 1, lambda j, c: scores(j, c, False), (neg, neg))
    mx = scores(nblk - 1, mx, True)
    m = [jnp.max(mx[hh], axis=-1, keepdims=True) for hh in range(2)]

    def weigh(j, carry):
        koff = pl.multiple_of(j * tk, tk)
        vb = v_ref[pl.ds(koff, tk), :]
        out = []
        for hh in range(2):
            ls, acc = carry[hh]
            p = jnp.exp(s_sc[hh, :, pl.ds(koff, tk)] - m[hh])
            for c in range(ntile):
                ls = ls + p[:, c * LANES:(c + 1) * LANES]
            out.append((ls, acc + _dot(p.astype(BF16), vb)))
        return tuple(out)

    zero = jnp.zeros((tq, LANES), F32)
    res = lax.fori_loop(0, nblk, weigh, ((zero, zero), (zero, zero)))
    outs = [res[hh][1] / jnp.sum(res[hh][0], axis=-1, keepdims=True) for hh in range(2)]
    o_ref[...] = _pair_select(lane, outs[0], outs[1]).astype(BF16)


def flash_prompt(q, k, v, batch, seq, *, fox, ccol=None, crow=None):
    n = batch * seq
    pairs = v.shape[1] // LANES
    wq = q.shape[1] // pairs
    tq = _row_tile(seq, 256)
    tk = 2 * tq if seq % (2 * tq) == 0 else tq
    nq = seq // tq
    in_specs = [
        pl.BlockSpec((tq, wq), lambda b, p, i: (b * nq + i, p)),
        pl.BlockSpec((seq, wq), lambda b, p, i: (b, p)),
        pl.BlockSpec((seq, LANES), lambda b, p, i: (b, p)),
    ]
    args = [q, k, v]
    if fox:
        in_specs += [
            pl.BlockSpec((1, 1, tq, 2), lambda b, p, i: (b, p, i, 0)),
            pl.BlockSpec((1, 1, 2, seq), lambda b, p, i: (b, p, 0, 0)),
        ]
        args += [ccol, crow]
    return pl.pallas_call(
        functools.partial(_flash_kernel, tq=tq, tk=tk, fox=fox),
        grid=(batch, pairs, nq),
        in_specs=in_specs,
        out_specs=pl.BlockSpec((tq, LANES), lambda b, p, i: (b * nq + i, p)),
        out_shape=jax.ShapeDtypeStruct((n, pairs * LANES), BF16),
        scratch_shapes=[pltpu.VMEM((2, tq, seq), F32)],
        compiler_params=_params("parallel", "parallel", "arbitrary"),
        name="flash_fox" if fox else "flash_mla",
    )(*args)


QW = MLA_KV_RANK + LANES


def _absorb_kernel(q_ref, w_ref, o_ref):
    r = _dot(q_ref[...], w_ref[0])
    o_ref[...] = r.reshape(o_ref.shape)


def mla_absorb(q, w_abs, bd, nq):
    n = q.shape[0]
    return pl.pallas_call(
        _absorb_kernel,
        grid=(MLA_HEADS,),
        in_specs=[pl.BlockSpec((n, LANES), lambda h: (0, h)),
                  pl.BlockSpec((1, LANES, QW), lambda h: (h, 0, 0))],
        out_specs=pl.BlockSpec((bd, 1, nq, QW), lambda h: (0, h, 0, 0)),
        out_shape=jax.ShapeDtypeStruct((bd, MLA_HEADS, nq, QW), F32),
        compiler_params=_params("parallel"),
        name="mla_absorb",
    )(q, w_abs)


def _online_update(s, weigh, m_sc, l_sc, acc_sc):
    m_old = m_sc[...]
    m_new = jnp.maximum(m_old, jnp.max(s, axis=-1, keepdims=True))
    corr = jnp.exp(m_old - m_new)
    p = jnp.exp(s - m_new)
    l_sc[...] = l_sc[...] * corr + jnp.sum(p, axis=-1, keepdims=True)
    acc_sc[...] = acc_sc[...] * corr + weigh(p.astype(BF16))
    m_sc[...] = m_new


def _softmax_init(m_sc, l_sc, acc_sc):
    m_sc[...] = jnp.full(m_sc.shape, NEG_INF, F32)
    l_sc[...] = jnp.zeros(l_sc.shape, F32)
    acc_sc[...] = jnp.zeros(acc_sc.shape, F32)


def _new_key_mask(rows, nq):
    r = lax.broadcasted_iota(jnp.int32, (rows, PAGE), 0)
    c = lax.broadcasted_iota(jnp.int32, (rows, PAGE), 1)
    return c <= (r % nq)


def _mla_sample_kernel(pt_ref, q_ref, *rest, pp, nq):
    ckv_refs = rest[:pp]
    kpe_refs = rest[pp:2 * pp]
    cn_ref, pn_ref, o_ref, m_sc, l_sc, acc_sc, kn_sc, rn_sc = rest[2 * pp:]
    j = pl.program_id(1)
    q = q_ref[0]
    ql = q[:, :MLA_KV_RANK].astype(BF16)
    qp = q[:, MLA_KV_RANK:MLA_KV_RANK + MLA_ROPE].astype(BF16)
    rows = q.shape[0]

    @pl.when(j == 0)
    def _():
        _softmax_init(m_sc, l_sc, acc_sc)
        kn_sc[...] = jnp.zeros(kn_sc.shape, F32)
        rn_sc[...] = jnp.zeros(rn_sc.shape, F32)
        kn_sc[0:nq, :] = cn_ref[0]
        rn_sc[0:nq, :] = pn_ref[0]
        kc = kn_sc[...].astype(BF16)
        s = _dot_nt(ql, kc) + _dot_nt(qp, rn_sc[...].astype(BF16))
        s = jnp.where(_new_key_mask(rows, nq), s, NEG_INF)
        _online_update(s, lambda p: _dot(p, kc), m_sc, l_sc, acc_sc)

    kc = jnp.concatenate([r[0, 0] for r in ckv_refs], axis=0).astype(BF16)
    kp = jnp.concatenate([r[0, 0] for r in kpe_refs], axis=1).astype(BF16)
    _online_update(_dot_nt(ql, kc) + _dot(qp, kp), lambda p: _dot(p, kc), m_sc, l_sc, acc_sc)

    @pl.when(j == pl.num_programs(1) - 1)
    def _():
        o_ref[0] = acc_sc[...] / l_sc[...]


def mla_sample_attention(qabs, cache_ckv, cache_kpe_t, layer, page_table, ckv_new, kpe_new, pp):
    bd, rows, _ = qabs.shape
    n_pages = page_table.shape[1]
    nq = ckv_new.shape[1]

    def page_spec(shape, t):
        return pl.BlockSpec((1, 1) + shape, lambda b, j, pt: (layer, pt[b, j * pp + t], 0, 0))

    in_specs = ([pl.BlockSpec((1, rows, QW), lambda b, j, pt: (b, 0, 0))]
                + [page_spec((PAGE, MLA_KV_RANK), t) for t in range(pp)]
                + [page_spec((MLA_ROPE, PAGE), t) for t in range(pp)]
                + [pl.BlockSpec((1, nq, MLA_KV_RANK), lambda b, j, pt: (b, 0, 0)),
                   pl.BlockSpec((1, nq, MLA_ROPE), lambda b, j, pt: (b, 0, 0))])
    grid_spec = pltpu.PrefetchScalarGridSpec(
        num_scalar_prefetch=1,
        grid=(bd, n_pages // pp),
        in_specs=in_specs,
        out_specs=pl.BlockSpec((1, rows, MLA_KV_RANK), lambda b, j, pt: (b, 0, 0)),
        scratch_shapes=[pltpu.VMEM((rows, 1), F32), pltpu.VMEM((rows, 1), F32),
                        pltpu.VMEM((rows, MLA_KV_RANK), F32),
                        pltpu.VMEM((PAGE, MLA_KV_RANK), F32), pltpu.VMEM((PAGE, MLA_ROPE), F32)],
    )
    return pl.pallas_call(
        functools.partial(_mla_sample_kernel, pp=pp, nq=nq),
        grid_spec=grid_spec,
        out_shape=jax.ShapeDtypeStruct((bd, rows, MLA_KV_RANK), F32),
        compiler_params=_params("parallel", "arbitrary"),
        name="mla_sample_attention",
    )(page_table, qabs, *([cache_ckv] * pp), *([cache_kpe_t] * pp), ckv_new, kpe_new)


def _unabsorb_kernel(ol_ref, w_ref, o_ref):
    n = o_ref.shape[0]
    a = ol_ref[:, 0].reshape(n, MLA_KV_RANK).astype(BF16)
    b = ol_ref[:, 1].reshape(n, MLA_KV_RANK).astype(BF16)
    o_ref[...] = (_dot(a, w_ref[0, 0]) + _dot(b, w_ref[0, 1])).astype(BF16)


def mla_unabsorb(o_lat, w_unabs, bd, nq):
    n = bd * nq
    pairs = MLA_HEADS // 2
    return pl.pallas_call(
        _unabsorb_kernel,
        grid=(pairs,),
        in_specs=[pl.BlockSpec((bd, 2, nq, MLA_KV_RANK), lambda p: (0, p, 0, 0)),
                  pl.BlockSpec((1, 2, MLA_KV_RANK, LANES), lambda p: (p, 0, 0, 0))],
        out_specs=pl.BlockSpec((n, LANES), lambda p: (0, p)),
        out_shape=jax.ShapeDtypeStruct((n, pairs * LANES), BF16),
        compiler_params=_params("parallel"),
        name="mla_unabsorb",
    )(o_lat, w_unabs)


GW = DSW_HEADS * DSW_HEAD_DIM
N_GROUPS = len(DSW_GROUPS)


def _dsw_proj_kernel(x_ref, g_ref, w_ref, qkv_ref, kv_ref):
    h = _rms(x_ref[...], g_ref[...]).astype(BF16)
    for g in range(N_GROUPS):
        r = _dot(h, w_ref[:, g * 3 * GW:(g + 1) * 3 * GW])
        qkv_ref[:, g * 3 * GW:(g + 1) * 3 * GW] = r.astype(BF16)
        kv_ref[:, g * 2 * GW:(g + 1) * 2 * GW] = r[:, GW:]


def dsw_project(x, g_mix, w_qkv):
    n, d = x.shape
    tm = _row_tile(n, 256)
    wq = N_GROUPS * 3 * GW
    return pl.pallas_call(
        _dsw_proj_kernel,
        grid=(n // tm,),
        in_specs=[pl.BlockSpec((tm, d), lambda i: (i, 0)),
                  pl.BlockSpec((1, d), lambda i: (0, 0)),
                  pl.BlockSpec((d, wq), lambda i: (0, 0))],
        out_specs=[pl.BlockSpec((tm, wq), lambda i: (i, 0)),
                   pl.BlockSpec((tm, N_GROUPS * 2 * GW), lambda i: (i, 0))],
        out_shape=[jax.ShapeDtypeStruct((n, wq), BF16),
                   jax.ShapeDtypeStruct((n, N_GROUPS * 2 * GW), F32)],
        compiler_params=_params("parallel"),
        name="dsw_project",
    )(x, g_mix, w_qkv)


def _rel_bucket(dist):
    max_exact = REL_BUCKETS // 2
    d = dist.astype(F32)
    log_b = max_exact + jnp.log(jnp.maximum(d, 1.0) / max_exact) / math.log(REL_MAX_DIST / max_exact) * (REL_BUCKETS - max_exact)
    log_b = jnp.minimum(log_b.astype(jnp.int32), REL_BUCKETS - 1)
    return jnp.where(dist < max_exact, dist, log_b)


def _dsw_prompt_kernel(q_ref, kc_ref, kp_ref, vc_ref, vp_ref, bias_ref, o_ref, lse_ref, *, has_prev):
    n = pl.program_id(2)
    qb = q_ref.shape[0]
    lane = lax.broadcasted_iota(jnp.int32, (qb, LANES), 1)
    for pr in range(DSW_HEADS // 2):
        sl = slice(pr * LANES, (pr + 1) * LANES)
        qf = q_ref[:, sl].astype(F32)
        kc, vc = kc_ref[:, sl], vc_ref[:, sl]
        o_pair, lse_pair = [], []
        for hh in range(2):
            head = 2 * pr + hh
            keep = (lane < HALF) if hh == 0 else (lane >= HALF)
            qh = jnp.where(keep, qf, 0.0).astype(BF16)
            sc = _dot_nt(qh, kc) + bias_ref[head, :, qb:]
            m = jnp.max(sc, axis=-1, keepdims=True)
            if has_prev:
                sp = _dot_nt(qh, kp_ref[:, sl]) + bias_ref[head, :, :qb]
                sp = jnp.where(n > 0, sp, NEG_INF)
                m = jnp.maximum(m, jnp.max(sp, axis=-1, keepdims=True))
            pc = jnp.exp(sc - m)
            l = jnp.sum(pc, axis=-1, keepdims=True)
            o = _dot(pc.astype(BF16), vc)
            if has_prev:
                pp_ = jnp.exp(sp - m)
                l = l + jnp.sum(pp_, axis=-1, keepdims=True)
                o = o + _dot(pp_.astype(BF16), vp_ref[:, sl])
            o_pair.append(o / l)
            lse_pair.append(jnp.broadcast_to(m + jnp.log(l), (qb, LANES)))
        o_ref[:, sl] = _pair_select(lane, o_pair[0], o_pair[1])
        lse_ref[:, sl] = _pair_select(lane, lse_pair[0], lse_pair[1])


def dsw_prompt_group(qkv, g, dil, bias, batch, seq):
    n = batch * seq
    sub = seq // dil
    qb = DSW_SPAN
    assert sub % qb == 0
    nb = sub // qb
    wq = N_GROUPS * 3 * GW
    blocks_per_row = wq // GW
    view = qkv.reshape(batch * sub, dil * wq)

    def col(kind):
        return lambda b, r, i: (b * nb + i, r * blocks_per_row + g * 3 + kind)

    def col_prev(kind):
        return lambda b, r, i: (b * nb + jnp.maximum(i - 1, 0), r * blocks_per_row + g * 3 + kind)

    blk = lambda f: pl.BlockSpec((qb, GW), f)
    out_blk = pl.BlockSpec((qb, GW), lambda b, r, i: (b * nb + i, r))
    o, lse = pl.pallas_call(
        functools.partial(_dsw_prompt_kernel, has_prev=nb > 1),
        grid=(batch, dil, nb),
        in_specs=[blk(col(0)), blk(col(1)), blk(col_prev(1)), blk(col(2)), blk(col_prev(2)),
                  pl.BlockSpec(bias.shape, lambda b, r, i: (0, 0, 0))],
        out_specs=[out_blk, out_blk],
        out_shape=[jax.ShapeDtypeStruct((batch * sub, dil * GW), F32)] * 2,
        compiler_params=_params("parallel", "parallel", "arbitrary"),
        name=f"dsw_prompt_g{g}",
    )(view, view, view, view, view, bias)
    return o.reshape(n, GW), lse.reshape(n, GW)


def _dsw_prompt_bias(rel_bias, g, dil):
    qb = DSW_SPAN
    step = jnp.arange(qb)[:, None] + DSW_SPAN - jnp.arange(2 * qb)[None, :]
    valid = (step >= 0) & (step <= DSW_SPAN)
    tab = rel_bias[:, g * DSW_HEADS:(g + 1) * DSW_HEADS]
    bias = tab[_rel_bucket(jnp.clip(step, 0, DSW_SPAN) * dil)].transpose(2, 0, 1)
    return jnp.where(valid[None], bias, NEG_INF).astype(F32)


def _dsw_merge_kernel(o0, o1, o2, l0, l1, l2, w_ref, r_ref, out_ref):
    a0, a1, a2 = l0[...], l1[...], l2[...]
    m = jnp.maximum(jnp.maximum(a0, a1), a2)
    e0, e1, e2 = jnp.exp(a0 - m), jnp.exp(a1 - m), jnp.exp(a2 - m)
    merged = (e0 * o0[...] + e1 * o1[...] + e2 * o2[...]) / (e0 + e1 + e2)
    out_ref[...] = r_ref[...] + _dot(merged.astype(BF16), w_ref[...])


def dsw_merge_out(outs, lses, w_o, res):
    n, d = res.shape
    tm = _row_tile(n, 512)
    row = lambda wd: pl.BlockSpec((tm, wd), lambda i: (i, 0))
    return pl.pallas_call(
        _dsw_merge_kernel,
        grid=(n // tm,),
        in_specs=[row(GW)] * 6 + [pl.BlockSpec((GW, d), lambda i: (0, 0)), row(d)],
        out_specs=row(d),
        out_shape=jax.ShapeDtypeStruct((n, d), F32),
        compiler_params=_params("parallel"),
        name="dsw_merge_out",
    )(*outs, *lses, w_o, res)


def _diag_pairs(x, lane, nq, store):
    heads = x.shape[0] // nq
    for pr in range(heads // 2):
        a = x[2 * pr * nq:(2 * pr + 1) * nq, pr * LANES:(pr + 1) * LANES]
        b = x[(2 * pr + 1) * nq:(2 * pr + 2) * nq, pr * LANES:(pr + 1) * LANES]
        store(pr, _pair_select(lane, a, b))


DSW_SPLIT = 2
HW = GW // DSW_SPLIT


def _dsw_sample_kernel(buf_ref, newt_ref, q_ref, bias_ref, biasn_ref, nb_ref, o_ref, lse_ref, *, wb, nq):
    lane = lax.broadcasted_iota(jnp.int32, (HW, LANES), 1)
    keep = LANES - nq
    ntile = wb // LANES
    for part in range(2):
        upper = pltpu.roll(newt_ref[0, part], keep, 1)
        for c in range(ntile - 1, -1, -1):
            cur = pltpu.roll(buf_ref[0, part, :, c * LANES:(c + 1) * LANES], keep, 1)
            nb_ref[0, part, :, c * LANES:(c + 1) * LANES] = jnp.where(lane < keep, cur, upper)
            upper = cur

    q = q_ref[0, 0]
    vt = buf_ref[0, 1].astype(BF16)
    vnt = newt_ref[0, 1].astype(BF16)
    s_old = _dot(q, buf_ref[0, 0].astype(BF16)) + bias_ref[0]
    s_new = _dot(q, newt_ref[0, 0].astype(BF16)) + biasn_ref[0]
    m = jnp.maximum(jnp.max(s_old, axis=-1, keepdims=True), jnp.max(s_new, axis=-1, keepdims=True))
    p_old = jnp.exp(s_old - m)
    p_new = jnp.exp(s_new - m)
    l = jnp.sum(p_old, axis=-1, keepdims=True) + jnp.sum(p_new, axis=-1, keepdims=True)
    o_bd = (_dot_nt(p_old.astype(BF16), vt) + _dot_nt(p_new.astype(BF16), vnt)) / l
    lse_bd = jnp.broadcast_to(m + jnp.log(l), o_bd.shape)
    lane_q = lax.broadcasted_iota(jnp.int32, (nq, LANES), 1)

    def store_o(pr, val):
        o_ref[0, :, pr * LANES:(pr + 1) * LANES] = val

    def store_l(pr, val):
        lse_ref[0, :, pr * LANES:(pr + 1) * LANES] = val

    _diag_pairs(o_bd, lane_q, nq, store_o)
    _diag_pairs(lse_bd, lane_q, nq, store_l)


def dsw_sample_group(buf_t, new_t, q_bd, bias, bias_new, nq):
    bd, _, _, wb = buf_t.shape
    rows = q_bd.shape[2]
    big = lambda w: pl.BlockSpec((1, 2, HW, w), lambda b, h: (b, 0, h, 0))
    out_small = pl.BlockSpec((1, nq, HW), lambda b, h: (b, 0, h))
    return pl.pallas_call(
        functools.partial(_dsw_sample_kernel, wb=wb, nq=nq),
        grid=(bd, DSW_SPLIT),
        in_specs=[
            big(wb), big(LANES),
            pl.BlockSpec((1, 1, rows, HW), lambda b, h: (b, h, 0, 0)),
            pl.BlockSpec((1, rows, wb), lambda b, h: (h, 0, 0)),
            pl.BlockSpec((1, rows, LANES), lambda b, h: (h, 0, 0)),
        ],
        out_specs=[big(wb), out_small, out_small],
        out_shape=[jax.ShapeDtypeStruct(buf_t.shape, F32),
                   jax.ShapeDtypeStruct((bd, nq, GW), F32),
                   jax.ShapeDtypeStruct((bd, nq, GW), F32)],
        compiler_params=_params("parallel", "parallel"),
        name=f"dsw_sample_w{wb}",
    )(buf_t, new_t, q_bd, bias, bias_new)


def _dsw_sample_bias(rel_bias, g, dil, wb, nq):
    tab = rel_bias[:, g * DSW_HEADS:(g + 1) * DSW_HEADS]
    qi = jnp.arange(nq)[:, None]

    def table(key_idx):
        delta = wb + qi - key_idx[None, :]
        step = delta // dil
        valid = (delta >= 0) & (delta % dil == 0) & (step <= DSW_SPAN)
        b = tab[_rel_bucket(jnp.clip(step, 0, DSW_SPAN) * dil)]
        b = jnp.where(valid[..., None], b, NEG_INF).transpose(2, 0, 1)
        return b.reshape(DSW_HEADS * nq, -1).astype(F32)

    return table(jnp.arange(wb)), table(wb + jnp.arange(PAGE))


def _block_diag_q(q, bd, nq, heads, dh):
    q4 = q.reshape(bd, nq, heads, dh)
    eye = jnp.eye(heads, dtype=q.dtype)
    return jnp.einsum("bqhd,hg->bhqgd", q4, eye).reshape(bd, heads * nq, heads * dh)


FW = FOX_HEADS * FOX_HEAD_DIM


def _log_sigmoid(z):
    return jnp.minimum(z, 0.0) - jnp.log1p(jnp.exp(-jnp.abs(z)))


def _fox_proj_kernel(x_ref, g_ref, w_ref, wf_ref, bf_ref, q_ref, kvb_ref, kv_ref, lf_ref):
    h = _rms(x_ref[...], g_ref[...]).astype(BF16)
    q_ref[...] = _dot(h, w_ref[:, :FW]).astype(BF16)
    kv = _dot(h, w_ref[:, FW:])
    kv_ref[...] = kv
    kvb_ref[...] = kv.astype(BF16)
    lf_ref[...] = _log_sigmoid(_dot(h, wf_ref[...]) + bf_ref[...])


def fox_project(x, g_mix, w_qkv, w_f, b_f):
    n, d = x.shape
    tm = _row_tile(n, 256)
    row = lambda wd: pl.BlockSpec((tm, wd), lambda i: (i, 0))
    full = lambda a: pl.BlockSpec(a.shape, lambda i: (0,) * a.ndim)
    return pl.pallas_call(
        _fox_proj_kernel,
        grid=(n // tm,),
        in_specs=[row(d), full(g_mix), full(w_qkv), full(w_f), full(b_f)],
        out_specs=[row(FW), row(2 * FW), row(2 * FW), row(LANES)],
        out_shape=[jax.ShapeDtypeStruct((n, FW), BF16),
                   jax.ShapeDtypeStruct((n, 2 * FW), BF16),
                   jax.ShapeDtypeStruct((n, 2 * FW), F32),
                   jax.ShapeDtypeStruct((n, LANES), F32)],
        compiler_params=_params("parallel"),
        name="fox_project",
    )(x, g_mix, w_qkv, w_f, b_f)


def _cumsum_kernel(lf_ref, tri_ref, c_ref):
    tri = tri_ref[...]
    carry = jnp.zeros((1, LANES), F32)
    for c in range(lf_ref.shape[1] // PAGE):
        hi, mid, lo = _split3(lf_ref[0, c * PAGE:(c + 1) * PAGE, :])
        y = _dot(tri, hi) + _dot(tri, mid) + _dot(tri, lo) + carry
        c_ref[0, c * PAGE:(c + 1) * PAGE, :] = y
        carry = y[PAGE - 1:PAGE, :]


def cumsum_rows(lf, batch, seq):
    tri = jnp.tril(jnp.ones((PAGE, PAGE), F32)).astype(BF16)
    return pl.pallas_call(
        _cumsum_kernel,
        grid=(batch,),
        in_specs=[pl.BlockSpec((1, seq, LANES), lambda b: (b, 0, 0)),
                  pl.BlockSpec((PAGE, PAGE), lambda b: (0, 0))],
        out_specs=pl.BlockSpec((1, seq, LANES), lambda b: (b, 0, 0)),
        out_shape=jax.ShapeDtypeStruct((batch, seq, LANES), F32),
        compiler_params=_params("parallel"),
        name="cumsum_rows",
    )(lf.reshape(batch, seq, LANES), tri)


def _fox_cum_kernel(pt_ref, *refs, pp):
    page_refs = refs[:pp]
    lfn_ref, triu_ref, ck_ref, cn_ref, carry_sc = refs[pp:]
    j = pl.program_id(1)
    triu = triu_ref[...]

    def lane_cumsum(x):
        hi, mid, lo = _split3(x)
        return _dot(hi, triu) + _dot(mid, triu) + _dot(lo, triu)

    @pl.when(j == 0)
    def _():
        carry_sc[...] = jnp.zeros(carry_sc.shape, F32)

    x = jnp.concatenate([r[0, 0] for r in page_refs], axis=0)
    w = lane_cumsum(x)
    carry = carry_sc[...]
    for t in range(pp):
        wt = w[t * FOX_HEADS:(t + 1) * FOX_HEADS, :]
        ck_ref[0, t] = wt + carry
        carry = carry + wt[:, PAGE - 1:PAGE]
    carry_sc[...] = carry

    @pl.when(j == pl.num_programs(1) - 1)
    def _():
        cn_ref[0] = lane_cumsum(lfn_ref[0]) + carry


def fox_past_cumsum(logf_t, layer, page_table, lf_new_t, pp):
    bd, n_pages = page_table.shape
    triu = jnp.triu(jnp.ones((PAGE, PAGE), F32)).astype(BF16)
    in_specs = ([pl.BlockSpec((1, 1, FOX_HEADS, PAGE), functools.partial(
                    lambda b, j, pt, t: (layer, pt[b, j * pp + t], 0, 0), t=t)) for t in range(pp)]
                + [pl.BlockSpec((1, FOX_HEADS, PAGE), lambda b, j, pt: (b, 0, 0)),
                   pl.BlockSpec((PAGE, PAGE), lambda b, j, pt: (0, 0))])
    grid_spec = pltpu.PrefetchScalarGridSpec(
        num_scalar_prefetch=1,
        grid=(bd, n_pages // pp),
        in_specs=in_specs,
        out_specs=[pl.BlockSpec((1, pp, FOX_HEADS, PAGE), lambda b, j, pt: (b, j, 0, 0)),
                   pl.BlockSpec((1, FOX_HEADS, PAGE), lambda b, j, pt: (b, 0, 0))],
        scratch_shapes=[pltpu.VMEM((FOX_HEADS, 1), F32)],
    )
    return pl.pallas_call(
        functools.partial(_fox_cum_kernel, pp=pp),
        grid_spec=grid_spec,
        out_shape=[jax.ShapeDtypeStruct((bd, n_pages, FOX_HEADS, PAGE), F32),
                   jax.ShapeDtypeStruct((bd, FOX_HEADS, PAGE), F32)],
        compiler_params=_params("parallel", "arbitrary"),
        name="fox_past_cumsum",
    )(page_table, *([logf_t] * pp), lf_new_t, triu)


def _expand_rows(c, nq):
    h = c.shape[0]
    return jnp.broadcast_to(c[:, None, :], (h, nq, c.shape[1])).reshape(h * nq, c.shape[1])


def _fox_sample_kernel(pt_ref, q_ref, *rest, pp, nq):
    kv_refs = rest[:pp]
    ck_ref, ccol_ref, cnt_ref, new_ref, o_ref, m_sc, l_sc, acc_sc, kn_sc = rest[pp:]
    j = pl.program_id(1)
    q = q_ref[0]
    rows = q.shape[0]
    ccol = ccol_ref[0]

    @pl.when(j == 0)
    def _():
        _softmax_init(m_sc, l_sc, acc_sc)
        kn_sc[...] = jnp.zeros(kn_sc.shape, F32)
        kn_sc[0:nq, :] = new_ref[0]
        kn = kn_sc[...]
        vals = kn[:, FW:].astype(BF16)
        s = _dot_nt(q, kn[:, :FW].astype(BF16)) + (ccol - _expand_rows(cnt_ref[0], nq))
        s = jnp.where(_new_key_mask(rows, nq), s, NEG_INF)
        _online_update(s, lambda p: _dot(p, vals), m_sc, l_sc, acc_sc)

    kt = jnp.concatenate([r[0, 0, 0] for r in kv_refs], axis=1).astype(BF16)
    vt = jnp.concatenate([r[0, 0, 1] for r in kv_refs], axis=1).astype(BF16)
    ck = jnp.concatenate([ck_ref[0, t] for t in range(pp)], axis=1)
    s = _dot(q, kt) + (ccol - _expand_rows(ck, nq))
    _online_update(s, lambda p: _dot_nt(p, vt), m_sc, l_sc, acc_sc)

    @pl.when(j == pl.num_programs(1) - 1)
    def _():
        o_bd = acc_sc[...] / l_sc[...]
        lane = lax.broadcasted_iota(jnp.int32, (nq, LANES), 1)

        def store(pr, val):
            o_ref[0, :, pr * LANES:(pr + 1) * LANES] = val

        _diag_pairs(o_bd, lane, nq, store)


def fox_sample_attention(q_bd, cache_kv_t, layer, page_table, c_keys, c_col, c_new_t, kv_new, pp):
    bd, rows, _ = q_bd.shape
    n_pages = page_table.shape[1]
    nq = kv_new.shape[1]
    in_specs = ([pl.BlockSpec((1, rows, FW), lambda b, j, pt: (b, 0, 0))]
                + [pl.BlockSpec((1, 1, 2, FW, PAGE), functools.partial(
                    lambda b, j, pt, t: (layer, pt[b, j * pp + t], 0, 0, 0), t=t)) for t in range(pp)]
                + [pl.BlockSpec((1, pp, FOX_HEADS, PAGE), lambda b, j, pt: (b, j, 0, 0)),
                   pl.BlockSpec((1, rows, 1), lambda b, j, pt: (b, 0, 0)),
                   pl.BlockSpec((1, FOX_HEADS, PAGE), lambda b, j, pt: (b, 0, 0)),
                   pl.BlockSpec((1, nq, 2 * FW), lambda b, j, pt: (b, 0, 0))])
    grid_spec = pltpu.PrefetchScalarGridSpec(
        num_scalar_prefetch=1,
        grid=(bd, n_pages // pp),
        in_specs=in_specs,
        out_specs=pl.BlockSpec((1, nq, FW), lambda b, j, pt: (b, 0, 0)),
        scratch_shapes=[pltpu.VMEM((rows, 1), F32), pltpu.VMEM((rows, 1), F32),
                        pltpu.VMEM((rows, FW), F32), pltpu.VMEM((PAGE, 2 * FW), F32)],
    )
    return pl.pallas_call(
        functools.partial(_fox_sample_kernel, pp=pp, nq=nq),
        grid_spec=grid_spec,
        out_shape=jax.ShapeDtypeStruct((bd, nq, FW), F32),
        compiler_params=_params("parallel", "arbitrary"),
        name="fox_sample_attention",
    )(page_table, q_bd, *([cache_kv_t] * pp), c_keys, c_col, c_new_t, kv_new)


def _mla_weights(w_dq, g_q, w_uq, w_dkv, g_kv, w_ukv, w_o):
    pad = LANES - MLA_NOPE - MLA_ROPE
    uq = w_uq.reshape(MLA_Q_RANK, MLA_HEADS, MLA_NOPE + MLA_ROPE)
    uq = jnp.pad(uq, ((0, 0), (0, 0), (0, pad))).reshape(MLA_Q_RANK, MLA_HEADS * LANES)
    ukv = w_ukv.reshape(MLA_KV_RANK, MLA_HEADS, MLA_NOPE + MLA_V)
    uk, uv = ukv[..., :MLA_NOPE], ukv[..., MLA_NOPE:]
    uk_pad = jnp.pad(uk, ((0, 0), (0, 0), (0, LANES - MLA_NOPE))).reshape(MLA_KV_RANK, MLA_HEADS * LANES)
    wdr = jnp.pad(w_dkv[:, MLA_KV_RANK:], ((0, 0), (ROPE_LO, pad)))
    w_abs = jnp.zeros((MLA_HEADS, LANES, QW), F32)
    w_abs = w_abs.at[:, :MLA_NOPE, :MLA_KV_RANK].set(uk.transpose(1, 2, 0))
    w_abs = w_abs.at[:, ROPE_LO:ROPE_LO + MLA_ROPE, MLA_KV_RANK:MLA_KV_RANK + MLA_ROPE].set(
        jnp.broadcast_to(jnp.eye(MLA_ROPE, dtype=F32), (MLA_HEADS, MLA_ROPE, MLA_ROPE)))
    uvh = uv.transpose(1, 0, 2).reshape(MLA_HEADS // 2, 2, MLA_KV_RANK, MLA_V)
    w_un = jnp.zeros((MLA_HEADS // 2, 2, MLA_KV_RANK, LANES), F32)
    w_un = w_un.at[:, 0, :, :MLA_V].set(uvh[:, 0]).at[:, 1, :, MLA_V:].set(uvh[:, 1])
    return {
        "wdq": w_dq.astype(BF16), "gq": g_q.reshape(1, -1), "wuq": uq.astype(BF16),
        "wdc": w_dkv[:, :MLA_KV_RANK].astype(BF16), "wdr": wdr.astype(BF16), "gkv": g_kv.reshape(1, -1),
        "wuk": uk_pad.astype(BF16), "wuv": uv.reshape(MLA_KV_RANK, MLA_HEADS * MLA_V).astype(BF16),
        "wabs": w_abs.astype(BF16), "wun": w_un.astype(BF16), "wo": w_o.astype(BF16),
    }


def kernel(x_prompt, x_sample, page_table, cache_mla_ckv, cache_mla_kpe, state_dsw_kv_g0, state_dsw_kv_g1, state_dsw_kv_g2, cache_fox_kv, cache_fox_logf, rel_bias, g_mix, g_ffn, g_final, w_ffn_gate, w_ffn_up, w_ffn_down, a_w_dq, a_g_q, a_w_uq, a_w_dkv, a_g_kv, a_w_ukv, a_w_o, b_w_qkv, b_w_o, c_w_qkv, c_w_f, c_b_f, c_w_o):
    batch, seq, d = x_prompt.shape
    bd, nq, _ = x_sample.shape
    n_pages = page_table.shape[1]
    past = n_pages * PAGE
    depth = g_mix.shape[0]
    n_p, n_s = batch * seq, bd * nq
    page_table = page_table.astype(jnp.int32)
    dsw_states = (state_dsw_kv_g0, state_dsw_kv_g1, state_dsw_kv_g2)

    xp = x_prompt.reshape(n_p, d)
    xs = x_sample.reshape(n_s, d)

    rope_p = _rope_tables(jnp.arange(seq))
    rope_s = _rope_tables(jnp.tile(past + jnp.arange(nq), bd))

    fox_logf_t = cache_fox_logf.transpose(0, 1, 3, 2)
    fox_kv_t = cache_fox_kv.transpose(0, 1, 3, 4, 5, 2).reshape(cache_fox_kv.shape[:2] + (2, FW, PAGE))
    mla_kpe_t = cache_mla_kpe.transpose(0, 1, 3, 2)

    mla_p, mla_s, dsw_p, dsw_s, fox_p, fox_s = [], [], [], [], [], []
    for i in range(depth):
        kind, j = i % 3, i // 3
        gm = g_mix[i].reshape(1, d)
        if kind == 0:
            w = _mla_weights(a_w_dq[j], a_g_q[j], a_w_uq[j], a_w_dkv[j], a_g_kv[j], a_w_ukv[j], a_w_o[j])
            q, kf, v, ckv, kpe = mla_project(xp, gm, w, rope_p, seq)
            o = flash_prompt(q, kf, v, batch, seq, fox=False)
            xp = matmul_residual(o, w["wo"], xp)
            mla_p.append((ckv.reshape(batch, seq, MLA_KV_RANK),
                          kpe[:, ROPE_LO:ROPE_LO + MLA_ROPE].reshape(batch, seq, MLA_ROPE)))
            q, _, _, ckv, kpe = mla_project(xs, gm, w, rope_s, n_s)
            ckv_new = ckv.reshape(bd, nq, MLA_KV_RANK)
            kpe_new = kpe[:, ROPE_LO:ROPE_LO + MLA_ROPE].reshape(bd, nq, MLA_ROPE)
            qabs = mla_absorb(q, w["wabs"], bd, nq).reshape(bd, MLA_HEADS * nq, QW)
            o_lat = mla_sample_attention(qabs, cache_mla_ckv, mla_kpe_t, j, page_table,
                                         ckv_new, kpe_new, pp=16 if n_pages % 16 == 0 else 1)
            o = mla_unabsorb(o_lat.reshape(bd, MLA_HEADS, nq, MLA_KV_RANK), w["wun"], bd, nq)
            xs = matmul_residual(o, w["wo"], xs)
            mla_s.append((ckv_new, kpe_new))
        elif kind == 1:
            qscale = jnp.tile(jnp.concatenate([jnp.full((GW,), DSW_SCALE, F32), jnp.ones((2 * GW,), F32)]), N_GROUPS)
            w_qkv = (b_w_qkv[j] * qscale[None, :]).astype(BF16)
            w_o = b_w_o[j].astype(BF16)
            qkv, kv32 = dsw_project(xp, gm, w_qkv)
            outs, lses, st = [], [], []
            for g, (win, dil) in enumerate(DSW_GROUPS):
                o_g, lse_g = dsw_prompt_group(qkv, g, dil, _dsw_prompt_bias(rel_bias, g, dil), batch, seq)
                outs.append(o_g)
                lses.append(lse_g)
                kv_g = kv32[:, g * 2 * GW:(g + 1) * 2 * GW].reshape(batch, seq, 2, DSW_HEADS, DSW_HEAD_DIM)
                st.append(kv_g[:, seq - min(win, seq):])
            xp = dsw_merge_out(outs, lses, w_o, xp)
            dsw_p.append(st)
            qkv, kv32 = dsw_project(xs, gm, w_qkv)
            outs, lses, st = [], [], []
            for g, (win, dil) in enumerate(DSW_GROUPS):
                buf = dsw_states[g][j]
                wb = buf.shape[1]
                buf_t = buf.transpose(0, 2, 3, 4, 1).reshape(bd, 2, GW, wb)
                qg = qkv[:, g * 3 * GW:g * 3 * GW + GW].reshape(bd * nq, DSW_SPLIT, HW)
                q_bd = jnp.stack([_block_diag_q(qg[:, s], bd, nq, DSW_HEADS // DSW_SPLIT, DSW_HEAD_DIM)
                                  for s in range(DSW_SPLIT)], axis=1)
                kv_new = kv32[:, g * 2 * GW:(g + 1) * 2 * GW].reshape(bd, nq, 2, GW)
                new_t = jnp.pad(kv_new.transpose(0, 2, 3, 1), ((0, 0), (0, 0), (0, 0), (0, LANES - nq)))
                bias, bias_new = _dsw_sample_bias(rel_bias, g, dil, wb, nq)
                rows = DSW_HEADS * nq // DSW_SPLIT
                nbuf_t, o_g, lse_g = dsw_sample_group(buf_t, new_t, q_bd, bias.reshape(DSW_SPLIT, rows, wb),
                                                      bias_new.reshape(DSW_SPLIT, rows, PAGE), nq)
                outs.append(o_g.reshape(n_s, GW))
                lses.append(lse_g.reshape(n_s, GW))
                st.append(nbuf_t.reshape(bd, 2, DSW_HEADS, DSW_HEAD_DIM, wb).transpose(0, 4, 1, 2, 3))
            xs = dsw_merge_out(outs, lses, w_o, xs)
            dsw_s.append(st)
        else:
            qscale = jnp.concatenate([jnp.full((FW,), FOX_SCALE, F32), jnp.ones((2 * FW,), F32)])
            w_qkv = (c_w_qkv[j] * qscale[None, :]).astype(BF16)
            w_f = jnp.pad(c_w_f[j], ((0, 0), (0, LANES - FOX_HEADS))).astype(BF16)
            b_f = jnp.pad(c_b_f[j], (0, LANES - FOX_HEADS)).reshape(1, LANES)
            w_o = c_w_o[j].astype(BF16)
            q, kvb, kv32, lf = fox_project(xp, gm, w_qkv, w_f, b_f)
            c = cumsum_rows(lf, batch, seq)[:, :, :FOX_HEADS]
            ccol = c.reshape(batch, seq, FOX_HEADS // 2, 2).transpose(0, 2, 1, 3)
            crow = c.transpose(0, 2, 1).reshape(batch, FOX_HEADS // 2, 2, seq)
            o = flash_prompt(q, kvb[:, :FW], kvb[:, FW:], batch, seq, fox=True, ccol=ccol, crow=crow)
            xp = matmul_residual(o, w_o, xp)
            fox_p.append((kv32.reshape(batch, seq, 2, FOX_HEADS, FOX_HEAD_DIM),
                          lf[:, :FOX_HEADS].reshape(batch, seq, FOX_HEADS)))
            q, _, kv32, lf = fox_project(xs, gm, w_qkv, w_f, b_f)
            lf_new = lf[:, :FOX_HEADS].reshape(bd, nq, FOX_HEADS)
            lf_new_t = jnp.pad(lf_new.transpose(0, 2, 1), ((0, 0), (0, 0), (0, PAGE - nq)))
            c_keys, c_new_t = fox_past_cumsum(fox_logf_t, j, page_table, lf_new_t,
                                              pp=16 if n_pages % 16 == 0 else 1)
            c_col = c_new_t[:, :, :nq].reshape(bd, FOX_HEADS * nq, 1)
            q_bd = _block_diag_q(q, bd, nq, FOX_HEADS, FOX_HEAD_DIM)
            o = fox_sample_attention(q_bd, fox_kv_t, j, page_table, c_keys, c_col, c_new_t,
                                     kv32.reshape(bd, nq, 2 * FW), pp=4 if n_pages % 4 == 0 else 1)
            xs = matmul_residual(o.reshape(n_s, FW), w_o, xs)
            fox_s.append((kv32.reshape(bd, nq, 2, FOX_HEADS, FOX_HEAD_DIM), lf_new))
        wg, wu, wd = w_ffn_gate[i].astype(BF16), w_ffn_up[i].astype(BF16), w_ffn_down[i].astype(BF16)
        xp = ffn_block(xp, g_ffn[i], wg, wu, wd)
        xs = ffn_block(xs, g_ffn[i], wg, wu, wd)

    y_prompt = final_norm(xp, g_final).reshape(batch, seq, d)
    y_sample = final_norm(xs, g_final).reshape(bd, nq, d)
    stack = lambda rows, k: jnp.stack([r[k] for r in rows])
    return (y_prompt, y_sample,
            stack(mla_p, 0), stack(mla_p, 1), stack(mla_s, 0), stack(mla_s, 1),
            stack(dsw_p, 0), stack(dsw_p, 1), stack(dsw_p, 2),
            stack(dsw_s, 0), stack(dsw_s, 1), stack(dsw_s, 2),
            stack(fox_p, 0), stack(fox_p, 1), stack(fox_s, 0), stack(fox_s, 1))
```

```python
import functools
import math

import jax
import jax.numpy as jnp
from jax import lax
from jax.experimental import pallas as pl
from jax.experimental.pallas import tpu as pltpu

F32 = jnp.float32
BF16 = jnp.bfloat16

D_MODEL = 1024
PAGE = 128
MLA_HEADS = 16
MLA_Q_RANK = 384
MLA_KV_RANK = 256
MLA_NOPE = 64
MLA_ROPE = 32
MLA_V = 64
ROPE_THETA = 10000.0
MLA_SCALE = (MLA_NOPE + MLA_ROPE) ** -0.5
DSW_GROUPS = ((128, 1), (512, 4), (2048, 16))
DSW_SPAN = 128
DSW_HEADS = 8
DSW_HEAD_DIM = 64
DSW_SCALE = DSW_HEAD_DIM ** -0.5
REL_BUCKETS = 32
REL_MAX_DIST = 2048
FOX_HEADS = 16
FOX_HEAD_DIM = 64
FOX_SCALE = FOX_HEAD_DIM ** -0.5
RMS_EPS = 1e-6
NEG_INF = -1e30

LANES = 128
SUBLANES = 8
HALF = LANES // 2
VMEM_LIMIT = 52 * 1024 * 1024

NT_DIMS = (((1,), (1,)), ((), ()))


def _params(*sem):
    return pltpu.CompilerParams(dimension_semantics=sem, vmem_limit_bytes=VMEM_LIMIT)


def _row_tile(n, pref):
    t = pref
    while t > SUBLANES and n % t:
        t //= 2
    assert n % t == 0, (n, pref)
    return t


def _dot(a, b):
    return jnp.dot(a, b, preferred_element_type=F32)


def _dot_nt(a, b):
    return lax.dot_general(a, b, NT_DIMS, preferred_element_type=F32)


def _rms(x, g):
    var = jnp.mean(x * x, axis=-1, keepdims=True)
    return x * lax.rsqrt(var + RMS_EPS) * g


def _split3(x):
    hi = x.astype(BF16)
    r1 = x - hi.astype(F32)
    mid = r1.astype(BF16)
    lo = (r1 - mid.astype(F32)).astype(BF16)
    return hi, mid, lo


def _pair_select(lane, a, b):
    return jnp.where(lane < HALF, a, b)


def _ffn_kernel(x_ref, g_ref, wg_ref, wu_ref, wd_ref, o_ref, h_sc, acc_sc):
    j = pl.program_id(1)

    @pl.when(j == 0)
    def _():
        x = x_ref[...]
        h_sc[...] = _rms(x, g_ref[...]).astype(BF16)
        acc_sc[...] = x

    h = h_sc[...]
    a = _dot(h, wg_ref[...])
    u = _dot(h, wu_ref[...])
    act = (a * jax.nn.sigmoid(a) * u).astype(BF16)
    acc_sc[...] += _dot(act, wd_ref[...])

    @pl.when(j == pl.num_programs(1) - 1)
    def _():
        o_ref[...] = acc_sc[...]


def ffn_block(x, g, wg, wu, wd):
    n, d = x.shape
    f = wg.shape[1]
    tm = _row_tile(n, 512)
    nf = 2 if f % (2 * LANES) == 0 else 1
    tf = f // nf
    return pl.pallas_call(
        _ffn_kernel,
        grid=(n // tm, nf),
        in_specs=[
            pl.BlockSpec((tm, d), lambda i, j: (i, 0)),
            pl.BlockSpec((1, d), lambda i, j: (0, 0)),
            pl.BlockSpec((d, tf), lambda i, j: (0, j)),
            pl.BlockSpec((d, tf), lambda i, j: (0, j)),
            pl.BlockSpec((tf, d), lambda i, j: (j, 0)),
        ],
        out_specs=pl.BlockSpec((tm, d), lambda i, j: (i, 0)),
        out_shape=jax.ShapeDtypeStruct((n, d), F32),
        scratch_shapes=[pltpu.VMEM((tm, d), BF16), pltpu.VMEM((tm, d), F32)],
        compiler_params=_params("parallel", "arbitrary"),
        name="ffn_block",
    )(x, g.reshape(1, d), wg, wu, wd)


def _matmul_res_kernel(a_ref, w_ref, r_ref, o_ref):
    o_ref[...] = r_ref[...] + _dot(a_ref[...].astype(BF16), w_ref[...])


def matmul_residual(a, w, res):
    n, k = a.shape
    d = w.shape[1]
    tm = _row_tile(n, 512)
    return pl.pallas_call(
        _matmul_res_kernel,
        grid=(n // tm,),
        in_specs=[
            pl.BlockSpec((tm, k), lambda i: (i, 0)),
            pl.BlockSpec((k, d), lambda i: (0, 0)),
            pl.BlockSpec((tm, d), lambda i: (i, 0)),
        ],
        out_specs=pl.BlockSpec((tm, d), lambda i: (i, 0)),
        out_shape=jax.ShapeDtypeStruct((n, d), F32),
        compiler_params=_params("parallel"),
        name="matmul_residual",
    )(a, w, res)


def _final_norm_kernel(x_ref, g_ref, o_ref):
    o_ref[...] = _rms(x_ref[...], g_ref[...])


def final_norm(x, g):
    n, d = x.shape
    tm = _row_tile(n, 1024)
    return pl.pallas_call(
        _final_norm_kernel,
        grid=(n // tm,),
        in_specs=[pl.BlockSpec((tm, d), lambda i: (i, 0)), pl.BlockSpec((1, d), lambda i: (0, 0))],
        out_specs=pl.BlockSpec((tm, d), lambda i: (i, 0)),
        out_shape=jax.ShapeDtypeStruct((n, d), F32),
        compiler_params=_params("parallel"),
        name="final_norm",
    )(x, g.reshape(1, d))


ROPE_LO = MLA_NOPE
ROPE_HALF = MLA_ROPE // 2


def _rope_tables(pos):
    inv = ROPE_THETA ** (-jnp.arange(ROPE_HALF, dtype=F32) / ROPE_HALF)
    ang = pos.astype(F32)[:, None] * inv
    cos, sin = jnp.cos(ang), jnp.sin(ang)
    n = pos.shape[0]
    ones = jnp.ones((n, ROPE_LO), F32)
    zl = jnp.zeros((n, ROPE_LO), F32)
    zh = jnp.zeros((n, ROPE_HALF), F32)
    zt = jnp.zeros((n, LANES - ROPE_LO - MLA_ROPE), F32)
    c = jnp.concatenate([ones, cos, cos, ones[:, : zt.shape[1]]], axis=1)
    s_up = jnp.concatenate([zl, zh, sin, zt], axis=1)
    s_dn = jnp.concatenate([zl, -sin, zh, zt], axis=1)
    return c, s_up, s_dn


def _rope_tile(t, cos, s_up, s_dn):
    return (t * cos + pltpu.roll(t, ROPE_HALF, 1) * s_up
            + pltpu.roll(t, LANES - ROPE_HALF, 1) * s_dn)


def _mla_proj_kernel(x_ref, gm_ref, wdq_ref, gq_ref, wuq_ref, wdc_ref, wdr_ref, gkv_ref,
                     wuk_ref, wuv_ref, cos_ref, sup_ref, sdn_ref,
                     q_ref, kf_ref, v_ref, ckv_ref, kpe_ref):
    cos, s_up, s_dn = cos_ref[...], sup_ref[...], sdn_ref[...]
    h = _rms(x_ref[...], gm_ref[...]).astype(BF16)
    qn = _rms(_dot(h, wdq_ref[...]), gq_ref[...]).astype(BF16)
    q = _dot(qn, wuq_ref[...])
    for hh in range(MLA_HEADS):
        sl = slice(hh * LANES, (hh + 1) * LANES)
        q_ref[:, sl] = (_rope_tile(q[:, sl], cos, s_up, s_dn) * MLA_SCALE).astype(BF16)
    ckv = _rms(_dot(h, wdc_ref[...]), gkv_ref[...])
    ckv_ref[...] = ckv
    kpe = _rope_tile(_dot(h, wdr_ref[...]), cos, s_up, s_dn)
    kpe_ref[...] = kpe
    cb = ckv.astype(BF16)
    kn = _dot(cb, wuk_ref[...])
    for hh in range(MLA_HEADS):
        sl = slice(hh * LANES, (hh + 1) * LANES)
        kf_ref[:, sl] = (kn[:, sl] + kpe).astype(BF16)
    v_ref[...] = _dot(cb, wuv_ref[...]).astype(BF16)


def mla_project(x, g_mix, w, tables, tab_blocks):
    n, d = x.shape
    tm = _row_tile(n, 256)
    tb = tab_blocks // tm if tab_blocks >= tm else 1
    hq = MLA_HEADS * LANES
    full = lambda a: pl.BlockSpec(a.shape, lambda i: (0,) * a.ndim)
    tab = pl.BlockSpec((tm, LANES), lambda i: (i % tb, 0))
    row = lambda wd: pl.BlockSpec((tm, wd), lambda i: (i, 0))
    ws = (w["wdq"], w["gq"], w["wuq"], w["wdc"], w["wdr"], w["gkv"], w["wuk"], w["wuv"])
    return pl.pallas_call(
        _mla_proj_kernel,
        grid=(n // tm,),
        in_specs=[row(d), full(g_mix)] + [full(a) for a in ws] + [tab, tab, tab],
        out_specs=[row(hq), row(hq), row(MLA_HEADS * MLA_V), row(MLA_KV_RANK), row(LANES)],
        out_shape=[
            jax.ShapeDtypeStruct((n, hq), BF16),
            jax.ShapeDtypeStruct((n, hq), BF16),
            jax.ShapeDtypeStruct((n, MLA_HEADS * MLA_V), BF16),
            jax.ShapeDtypeStruct((n, MLA_KV_RANK), F32),
            jax.ShapeDtypeStruct((n, LANES), F32),
        ],
        compiler_params=_params("parallel"),
        name="mla_project",
    )(x, g_mix, *ws, *tables)


def _flash_kernel(*refs, tq, tk, fox):
    if fox:
        q_ref, k_ref, v_ref, ccol_ref, crow_ref, o_ref, s_sc = refs
    else:
        q_ref, k_ref, v_ref, o_ref, s_sc = refs
    qi = pl.program_id(2)
    nblk = (qi * tq) // tk + 1
    ntile = tk // LANES
    lane = lax.broadcasted_iota(jnp.int32, (tq, LANES), 1)
    row = qi * tq + lax.broadcasted_iota(jnp.int32, (tq, tk), 0)
    col = lax.broadcasted_iota(jnp.int32, (tq, tk), 1)
    if fox:
        qf = q_ref[...].astype(F32)
        qh = [jnp.where(lane < HALF, qf, 0.0).astype(BF16), jnp.where(lane >= HALF, qf, 0.0).astype(BF16)]
        kcols = [slice(0, LANES)] * 2
    else:
        qh = [q_ref[:, :LANES], q_ref[:, LANES:2 * LANES]]
        kcols = [slice(0, LANES), slice(LANES, 2 * LANES)]

    def scores(j, mx, masked):
        koff = pl.multiple_of(j * tk, tk)
        out = []
        for hh in range(2):
            s = _dot_nt(qh[hh], k_ref[pl.ds(koff, tk), kcols[hh]])
            if fox:
                s = s + (ccol_ref[0, 0, :, hh:hh + 1] - crow_ref[0, 0, hh:hh + 1, pl.ds(koff, tk)])
            if masked:
                s = jnp.where(col + koff <= row, s, NEG_INF)
            s_sc[hh, :, pl.ds(koff, tk)] = s
            m = mx[hh]
            for c in range(ntile):
                m = jnp.maximum(m, s[:, c * LANES:(c + 1) * LANES])
            out.append(m)
        return tuple(out)

    neg = jnp.full((tq, LANES), NEG_INF, F32)
    mx = lax.fori_loop(0, nblk - 1, lambda j, c: scores(j, c, False), (neg, neg))
    mx = scores(nblk - 1, mx, True)
    m = [jnp.max(mx[hh], axis=-1, keepdims=True) for hh in range(2)]

    def weigh(j, carry):
        koff = pl.multiple_of(j * tk, tk)
        vb = v_ref[pl.ds(koff, tk), :]
        out = []
        for hh in range(2):
            ls, acc = carry[hh]
            p = jnp.exp(s_sc[hh, :, pl.ds(koff, tk)] - m[hh])
            for c in range(ntile):
                ls = ls + p[:, c * LANES:(c + 1) * LANES]
            out.append((ls, acc + _dot(p.astype(BF16), vb)))
        return tuple(out)

    zero = jnp.zeros((tq, LANES), F32)
    res = lax.fori_loop(0, nblk, weigh, ((zero, zero), (zero, zero)))
    outs = [res[hh][1] / jnp.sum(res[hh][0], axis=-1, keepdims=True) for hh in range(2)]
    o_ref[...] = _pair_select(lane, outs[0], outs[1]).astype(BF16)


def flash_prompt(q, k, v, batch, seq, *, fox, ccol=None, crow=None):
    n = batch * seq
    pairs = v.shape[1] // LANES
    wq = q.shape[1] // pairs
    tq = _row_tile(seq, 512)
    tk = tq
    nq = seq // tq
    in_specs = [
        pl.BlockSpec((tq, wq), lambda b, p, i: (b * nq + i, p)),
        pl.BlockSpec((seq, wq), lambda b, p, i: (b, p)),
        pl.BlockSpec((seq, LANES), lambda b, p, i: (b, p)),
    ]
    args = [q, k, v]
    if fox:
        in_specs += [
            pl.BlockSpec((1, 1, tq, 2), lambda b, p, i: (b, p, i, 0)),
            pl.BlockSpec((1, 1, 2, seq), lambda b, p, i: (b, p, 0, 0)),
        ]
        args += [ccol, crow]
    return pl.pallas_call(
        functools.partial(_flash_kernel, tq=tq, tk=tk, fox=fox),
        grid=(batch, pairs, nq),
        in_specs=in_specs,
        out_specs=pl.BlockSpec((tq, LANES), lambda b, p, i: (b * nq + i, p)),
        out_shape=jax.ShapeDtypeStruct((n, pairs * LANES), BF16),
        scratch_shapes=[pltpu.VMEM((2, tq, seq), F32)],
        compiler_params=_params("parallel", "parallel", "arbitrary"),
        name="flash_fox" if fox else "flash_mla",
    )(*args)


QW = MLA_KV_RANK + LANES


def _absorb_kernel(q_ref, w_ref, o_ref):
    r = _dot(q_ref[...], w_ref[0])
    o_ref[...] = r.reshape(o_ref.shape)


def mla_absorb(q, w_abs, bd, nq):
    n = q.shape[0]
    return pl.pallas_call(
        _absorb_kernel,
        grid=(MLA_HEADS,),
        in_specs=[pl.BlockSpec((n, LANES), lambda h: (0, h)),
                  pl.BlockSpec((1, LANES, QW), lambda h: (h, 0, 0))],
        out_specs=pl.BlockSpec((bd, 1, nq, QW), lambda h: (0, h, 0, 0)),
        out_shape=jax.ShapeDtypeStruct((bd, MLA_HEADS, nq, QW), F32),
        compiler_params=_params("parallel"),
        name="mla_absorb",
    )(q, w_abs)


def _online_update(s, weigh, m_sc, l_sc, acc_sc):
    m_old = m_sc[...]
    m_new = jnp.maximum(m_old, jnp.max(s, axis=-1, keepdims=True))
    corr = jnp.exp(m_old - m_new)
    p = jnp.exp(s - m_new)
    l_sc[...] = l_sc[...] * corr + jnp.sum(p, axis=-1, keepdims=True)
    acc_sc[...] = acc_sc[...] * corr + weigh(p.astype(BF16))
    m_sc[...] = m_new


def _softmax_init(m_sc, l_sc, acc_sc):
    m_sc[...] = jnp.full(m_sc.shape, NEG_INF, F32)
    l_sc[...] = jnp.zeros(l_sc.shape, F32)
    acc_sc[...] = jnp.zeros(acc_sc.shape, F32)


def _new_key_mask(rows, nq):
    r = lax.broadcasted_iota(jnp.int32, (rows, PAGE), 0)
    c = lax.broadcasted_iota(jnp.int32, (rows, PAGE), 1)
    return c <= (r % nq)


def _mla_sample_kernel(pt_ref, q_ref, *rest, pp, nq, eb):
    ckv_refs = rest[:eb * pp]
    kpe_refs = rest[eb * pp:2 * eb * pp]
    cn_ref, pn_ref, o_ref, m_sc, l_sc, acc_sc, kn_sc, rn_sc = rest[2 * eb * pp:]
    j = pl.program_id(1)
    rows = q_ref.shape[1]

    def queries(e):
        q = q_ref[e]
        return q[:, :MLA_KV_RANK].astype(BF16), q[:, MLA_KV_RANK:MLA_KV_RANK + MLA_ROPE].astype(BF16)

    @pl.when(j == 0)
    def _():
        for e in range(eb):
            ql, qp = queries(e)
            _softmax_init(m_sc.at[e], l_sc.at[e], acc_sc.at[e])
            kn_sc[...] = jnp.zeros(kn_sc.shape, F32)
            rn_sc[...] = jnp.zeros(rn_sc.shape, F32)
            kn_sc[0:nq, :] = cn_ref[e]
            rn_sc[0:nq, :] = pn_ref[e]
            kc = kn_sc[...].astype(BF16)
            s = _dot_nt(ql, kc) + _dot_nt(qp, rn_sc[...].astype(BF16))
            s = jnp.where(_new_key_mask(rows, nq), s, NEG_INF)
            _online_update(s, lambda p, kc=kc: _dot(p, kc), m_sc.at[e], l_sc.at[e], acc_sc.at[e])

    for e in range(eb):
        ql, qp = queries(e)
        kc = jnp.concatenate([r[0, 0] for r in ckv_refs[e * pp:(e + 1) * pp]], axis=0).astype(BF16)
        kp = jnp.concatenate([r[0, 0] for r in kpe_refs[e * pp:(e + 1) * pp]], axis=1).astype(BF16)
        _online_update(_dot_nt(ql, kc) + _dot(qp, kp), lambda p, kc=kc: _dot(p, kc),
                       m_sc.at[e], l_sc.at[e], acc_sc.at[e])

    @pl.when(j == pl.num_programs(1) - 1)
    def _():
        for e in range(eb):
            o_ref[e] = acc_sc[e] / l_sc[e]


def mla_sample_attention(qabs, cache_ckv, cache_kpe_t, layer, page_table, ckv_new, kpe_new, pp, eb):
    bd, rows, _ = qabs.shape
    n_pages = page_table.shape[1]
    nq = ckv_new.shape[1]

    def page_spec(shape, e, t):
        return pl.BlockSpec((1, 1) + shape, lambda b, j, pt: (layer, pt[b * eb + e, j * pp + t], 0, 0))

    per_seq = lambda w: pl.BlockSpec((eb, w[0], w[1]), lambda b, j, pt: (b, 0, 0))
    in_specs = ([per_seq((rows, QW))]
                + [page_spec((PAGE, MLA_KV_RANK), e, t) for e in range(eb) for t in range(pp)]
                + [page_spec((MLA_ROPE, PAGE), e, t) for e in range(eb) for t in range(pp)]
                + [per_seq((nq, MLA_KV_RANK)), per_seq((nq, MLA_ROPE))])
    grid_spec = pltpu.PrefetchScalarGridSpec(
        num_scalar_prefetch=1,
        grid=(bd // eb, n_pages // pp),
        in_specs=in_specs,
        out_specs=per_seq((rows, MLA_KV_RANK)),
        scratch_shapes=[pltpu.VMEM((eb, rows, 1), F32), pltpu.VMEM((eb, rows, 1), F32),
                        pltpu.VMEM((eb, rows, MLA_KV_RANK), F32),
                        pltpu.VMEM((PAGE, MLA_KV_RANK), F32), pltpu.VMEM((PAGE, MLA_ROPE), F32)],
    )
    return pl.pallas_call(
        functools.partial(_mla_sample_kernel, pp=pp, nq=nq, eb=eb),
        grid_spec=grid_spec,
        out_shape=jax.ShapeDtypeStruct((bd, rows, MLA_KV_RANK), F32),
        compiler_params=_params("parallel", "arbitrary"),
        name="mla_sample_attention",
    )(page_table, qabs, *([cache_ckv] * (eb * pp)), *([cache_kpe_t] * (eb * pp)), ckv_new, kpe_new)


def _unabsorb_kernel(ol_ref, w_ref, o_ref):
    n = o_ref.shape[0]
    a = ol_ref[:, 0].reshape(n, MLA_KV_RANK).astype(BF16)
    b = ol_ref[:, 1].reshape(n, MLA_KV_RANK).astype(BF16)
    o_ref[...] = (_dot(a, w_ref[0, 0]) + _dot(b, w_ref[0, 1])).astype(BF16)


def mla_unabsorb(o_lat, w_unabs, bd, nq):
    n = bd * nq
    pairs = MLA_HEADS // 2
    return pl.pallas_call(
        _unabsorb_kernel,
        grid=(pairs,),
        in_specs=[pl.BlockSpec((bd, 2, nq, MLA_KV_RANK), lambda p: (0, p, 0, 0)),
                  pl.BlockSpec((1, 2, MLA_KV_RANK, LANES), lambda p: (p, 0, 0, 0))],
        out_specs=pl.BlockSpec((n, LANES), lambda p: (0, p)),
        out_shape=jax.ShapeDtypeStruct((n, pairs * LANES), BF16),
        compiler_params=_params("parallel"),
        name="mla_unabsorb",
    )(o_lat, w_unabs)


GW = DSW_HEADS * DSW_HEAD_DIM
N_GROUPS = len(DSW_GROUPS)


def _dsw_proj_kernel(x_ref, g_ref, w_ref, qkv_ref, kv_ref):
    h = _rms(x_ref[...], g_ref[...]).astype(BF16)
    for g in range(N_GROUPS):
        r = _dot(h, w_ref[:, g * 3 * GW:(g + 1) * 3 * GW])
        qkv_ref[:, g * 3 * GW:(g + 1) * 3 * GW] = r.astype(BF16)
        kv_ref[:, g * 2 * GW:(g + 1) * 2 * GW] = r[:, GW:]


def dsw_project(x, g_mix, w_qkv):
    n, d = x.shape
    tm = _row_tile(n, 256)
    wq = N_GROUPS * 3 * GW
    return pl.pallas_call(
        _dsw_proj_kernel,
        grid=(n // tm,),
        in_specs=[pl.BlockSpec((tm, d), lambda i: (i, 0)),
                  pl.BlockSpec((1, d), lambda i: (0, 0)),
                  pl.BlockSpec((d, wq), lambda i: (0, 0))],
        out_specs=[pl.BlockSpec((tm, wq), lambda i: (i, 0)),
                   pl.BlockSpec((tm, N_GROUPS * 2 * GW), lambda i: (i, 0))],
        out_shape=[jax.ShapeDtypeStruct((n, wq), BF16),
                   jax.ShapeDtypeStruct((n, N_GROUPS * 2 * GW), F32)],
        compiler_params=_params("parallel"),
        name="dsw_project",
    )(x, g_mix, w_qkv)


def _dsw_proj_sub_kernel(x_ref, g_ref, w_ref, q0_ref, q1_ref, q2_ref, kv_ref, r_sc):
    h = _rms(x_ref[...], g_ref[...]).astype(BF16)
    tm = x_ref.shape[0]
    for g, (out_ref, (_, dil)) in enumerate(zip((q0_ref, q1_ref, q2_ref), DSW_GROUPS)):
        r = _dot(h, w_ref[:, g * 3 * GW:(g + 1) * 3 * GW])
        kv_ref[:, g * 2 * GW:(g + 1) * 2 * GW] = r[:, GW:]
        if dil == 1:
            out_ref[0, 0] = r.astype(BF16)
        else:
            for c in range(3 * GW // LANES):
                r_sc[c] = r[:, c * LANES:(c + 1) * LANES]
            for res in range(dil):
                for c in range(3 * GW // LANES):
                    out_ref[0, res, :, c * LANES:(c + 1) * LANES] = (
                        r_sc[c, pl.ds(res, tm // dil, stride=dil), :].astype(BF16))


def dsw_project_prompt(x, g_mix, w_qkv, batch, seq):
    n, d = x.shape
    tm = _row_tile(seq, 256)
    tpb = seq // tm
    wq = N_GROUPS * 3 * GW
    sub_spec = lambda dil: pl.BlockSpec((1, dil, tm // dil, 3 * GW), lambda i: (i // tpb, 0, i % tpb, 0))
    sub_shape = lambda dil: jax.ShapeDtypeStruct((batch, dil, seq // dil, 3 * GW), BF16)
    return pl.pallas_call(
        _dsw_proj_sub_kernel,
        grid=(n // tm,),
        in_specs=[pl.BlockSpec((tm, d), lambda i: (i, 0)),
                  pl.BlockSpec((1, d), lambda i: (0, 0)),
                  pl.BlockSpec((d, wq), lambda i: (0, 0))],
        out_specs=[sub_spec(dil) for _, dil in DSW_GROUPS]
                  + [pl.BlockSpec((tm, N_GROUPS * 2 * GW), lambda i: (i, 0))],
        out_shape=[sub_shape(dil) for _, dil in DSW_GROUPS]
                  + [jax.ShapeDtypeStruct((n, N_GROUPS * 2 * GW), F32)],
        scratch_shapes=[pltpu.VMEM((3 * GW // LANES, tm, LANES), F32)],
        compiler_params=_params("parallel"),
        name="dsw_project_prompt",
    )(x, g_mix, w_qkv)


def _rel_bucket(dist):
    max_exact = REL_BUCKETS // 2
    d = dist.astype(F32)
    log_b = max_exact + jnp.log(jnp.maximum(d, 1.0) / max_exact) / math.log(REL_MAX_DIST / max_exact) * (REL_BUCKETS - max_exact)
    log_b = jnp.minimum(log_b.astype(jnp.int32), REL_BUCKETS - 1)
    return jnp.where(dist < max_exact, dist, log_b)


def _dsw_prompt_kernel(q_ref, kc_ref, kp_ref, vc_ref, vp_ref, bias_ref, o_ref, lse_ref, *, has_prev):
    n = pl.program_id(2)
    qb = q_ref.shape[0]
    lane = lax.broadcasted_iota(jnp.int32, (qb, LANES), 1)
    for pr in range(DSW_HEADS // 2):
        sl = slice(pr * LANES, (pr + 1) * LANES)
        qf = q_ref[:, sl].astype(F32)
        kc, vc = kc_ref[:, sl], vc_ref[:, sl]
        o_pair, lse_pair = [], []
        for hh in range(2):
            head = 2 * pr + hh
            keep = (lane < HALF) if hh == 0 else (lane >= HALF)
            qh = jnp.where(keep, qf, 0.0).astype(BF16)
            sc = _dot_nt(qh, kc) + bias_ref[head, :, qb:]
            m = jnp.max(sc, axis=-1, keepdims=True)
            if has_prev:
                sp = _dot_nt(qh, kp_ref[:, sl]) + bias_ref[head, :, :qb]
                sp = jnp.where(n > 0, sp, NEG_INF)
                m = jnp.maximum(m, jnp.max(sp, axis=-1, keepdims=True))
            pc = jnp.exp(sc - m)
            l = jnp.sum(pc, axis=-1, keepdims=True)
            o = _dot(pc.astype(BF16), vc)
            if has_prev:
                pp_ = jnp.exp(sp - m)
                l = l + jnp.sum(pp_, axis=-1, keepdims=True)
                o = o + _dot(pp_.astype(BF16), vp_ref[:, sl])
            o_pair.append(o / l)
            lse_pair.append(jnp.broadcast_to(m + jnp.log(l), (qb, LANES)))
        o_ref[:, sl] = _pair_select(lane, o_pair[0], o_pair[1])
        lse_ref[:, sl] = _pair_select(lane, lse_pair[0], lse_pair[1])


def dsw_prompt_group(view, g, dil, bias, batch, seq):
    n = batch * seq
    sub = seq // dil
    qb = DSW_SPAN
    assert sub % qb == 0
    nb = sub // qb

    def col(kind):
        return lambda b, r, i: (b, r, i, kind)

    def col_prev(kind):
        return lambda b, r, i: (b, r, jnp.maximum(i - 1, 0), kind)

    blk = lambda f: pl.BlockSpec((None, None, qb, GW), f)
    out_blk = pl.BlockSpec((qb, GW), lambda b, r, i: (b * nb + i, r))
    o, lse = pl.pallas_call(
        functools.partial(_dsw_prompt_kernel, has_prev=nb > 1),
        grid=(batch, dil, nb),
        in_specs=[blk(col(0)), blk(col(1)), blk(col_prev(1)), blk(col(2)), blk(col_prev(2)),
                  pl.BlockSpec(bias.shape, lambda b, r, i: (0, 0, 0))],
        out_specs=[out_blk, out_blk],
        out_shape=[jax.ShapeDtypeStruct((batch * sub, dil * GW), F32)] * 2,
        compiler_params=_params("parallel", "parallel", "arbitrary"),
        name=f"dsw_prompt_g{g}",
    )(view, view, view, view, view, bias)
    return o.reshape(n, GW), lse.reshape(n, GW)


def _dsw_prompt_bias(rel_bias, g, dil):
    qb = DSW_SPAN
    step = jnp.arange(qb)[:, None] + DSW_SPAN - jnp.arange(2 * qb)[None, :]
    valid = (step >= 0) & (step <= DSW_SPAN)
    tab = rel_bias[:, g * DSW_HEADS:(g + 1) * DSW_HEADS]
    bias = tab[_rel_bucket(jnp.clip(step, 0, DSW_SPAN) * dil)].transpose(2, 0, 1)
    return jnp.where(valid[None], bias, NEG_INF).astype(F32)


def _dsw_merge_kernel(o0, o1, o2, l0, l1, l2, w_ref, r_ref, out_ref):
    a0, a1, a2 = l0[...], l1[...], l2[...]
    m = jnp.maximum(jnp.maximum(a0, a1), a2)
    e0, e1, e2 = jnp.exp(a0 - m), jnp.exp(a1 - m), jnp.exp(a2 - m)
    merged = (e0 * o0[...] + e1 * o1[...] + e2 * o2[...]) / (e0 + e1 + e2)
    out_ref[...] = r_ref[...] + _dot(merged.astype(BF16), w_ref[...])


def dsw_merge_out(outs, lses, w_o, res):
    n, d = res.shape
    tm = _row_tile(n, 512)
    row = lambda wd: pl.BlockSpec((tm, wd), lambda i: (i, 0))
    return pl.pallas_call(
        _dsw_merge_kernel,
        grid=(n // tm,),
        in_specs=[row(GW)] * 6 + [pl.BlockSpec((GW, d), lambda i: (0, 0)), row(d)],
        out_specs=row(d),
        out_shape=jax.ShapeDtypeStruct((n, d), F32),
        compiler_params=_params("parallel"),
        name="dsw_merge_out",
    )(*outs, *lses, w_o, res)


def _diag_pairs(x, lane, nq, store):
    heads = x.shape[0] // nq
    for pr in range(heads // 2):
        a = x[2 * pr * nq:(2 * pr + 1) * nq, pr * LANES:(pr + 1) * LANES]
        b = x[(2 * pr + 1) * nq:(2 * pr + 2) * nq, pr * LANES:(pr + 1) * LANES]
        store(pr, _pair_select(lane, a, b))


DSW_SPLIT = 2
HW = GW // DSW_SPLIT


def _dsw_sample_kernel(buf_ref, newt_ref, q_ref, bias_ref, biasn_ref, nb_ref, o_ref, lse_ref, *, wb, nq):
    lane = lax.broadcasted_iota(jnp.int32, (HW, LANES), 1)
    keep = LANES - nq
    ntile = wb // LANES
    for part in range(2):
        upper = pltpu.roll(newt_ref[0, part], keep, 1)
        for c in range(ntile - 1, -1, -1):
            cur = pltpu.roll(buf_ref[0, part, :, c * LANES:(c + 1) * LANES], keep, 1)
            nb_ref[0, part, :, c * LANES:(c + 1) * LANES] = jnp.where(lane < keep, cur, upper)
            upper = cur

    q = q_ref[0, 0]
    vt = buf_ref[0, 1].astype(BF16)
    vnt = newt_ref[0, 1].astype(BF16)
    s_old = _dot(q, buf_ref[0, 0].astype(BF16)) + bias_ref[0]
    s_new = _dot(q, newt_ref[0, 0].astype(BF16)) + biasn_ref[0]
    m = jnp.maximum(jnp.max(s_old, axis=-1, keepdims=True), jnp.max(s_new, axis=-1, keepdims=True))
    p_old = jnp.exp(s_old - m)
    p_new = jnp.exp(s_new - m)
    l = jnp.sum(p_old, axis=-1, keepdims=True) + jnp.sum(p_new, axis=-1, keepdims=True)
    o_bd = (_dot_nt(p_old.astype(BF16), vt) + _dot_nt(p_new.astype(BF16), vnt)) / l
    lse_bd = jnp.broadcast_to(m + jnp.log(l), o_bd.shape)
    lane_q = lax.broadcasted_iota(jnp.int32, (nq, LANES), 1)

    def store_o(pr, val):
        o_ref[0, :, pr * LANES:(pr + 1) * LANES] = val

    def store_l(pr, val):
        lse_ref[0, :, pr * LANES:(pr + 1) * LANES] = val

    _diag_pairs(o_bd, lane_q, nq, store_o)
    _diag_pairs(lse_bd, lane_q, nq, store_l)


def dsw_sample_group(buf_t, new_t, q_bd, bias, bias_new, nq):
    bd, _, _, wb = buf_t.shape
    rows = q_bd.shape[2]
    big = lambda w: pl.BlockSpec((1, 2, HW, w), lambda b, h: (b, 0, h, 0))
    out_small = pl.BlockSpec((1, nq, HW), lambda b, h: (b, 0, h))
    return pl.pallas_call(
        functools.partial(_dsw_sample_kernel, wb=wb, nq=nq),
        grid=(bd, DSW_SPLIT),
        in_specs=[
            big(wb), big(LANES),
            pl.BlockSpec((1, 1, rows, HW), lambda b, h: (b, h, 0, 0)),
            pl.BlockSpec((1, rows, wb), lambda b, h: (h, 0, 0)),
            pl.BlockSpec((1, rows, LANES), lambda b, h: (h, 0, 0)),
        ],
        out_specs=[big(wb), out_small, out_small],
        out_shape=[jax.ShapeDtypeStruct(buf_t.shape, F32),
                   jax.ShapeDtypeStruct((bd, nq, GW), F32),
                   jax.ShapeDtypeStruct((bd, nq, GW), F32)],
        compiler_params=_params("parallel", "parallel"),
        name=f"dsw_sample_w{wb}",
    )(buf_t, new_t, q_bd, bias, bias_new)


def _dsw_sample_bias(rel_bias, g, dil, wb, nq):
    tab = rel_bias[:, g * DSW_HEADS:(g + 1) * DSW_HEADS]
    qi = jnp.arange(nq)[:, None]

    def table(key_idx):
        delta = wb + qi - key_idx[None, :]
        step = delta // dil
        valid = (delta >= 0) & (delta % dil == 0) & (step <= DSW_SPAN)
        b = tab[_rel_bucket(jnp.clip(step, 0, DSW_SPAN) * dil)]
        b = jnp.where(valid[..., None], b, NEG_INF).transpose(2, 0, 1)
        return b.reshape(DSW_HEADS * nq, -1).astype(F32)

    return table(jnp.arange(wb)), table(wb + jnp.arange(PAGE))


def _block_diag_q(q, bd, nq, heads, dh):
    q4 = q.reshape(bd, nq, heads, dh)
    eye = jnp.eye(heads, dtype=q.dtype)
    return jnp.einsum("bqhd,hg->bhqgd", q4, eye).reshape(bd, heads * nq, heads * dh)


FW = FOX_HEADS * FOX_HEAD_DIM


def _log_sigmoid(z):
    return jnp.minimum(z, 0.0) - jnp.log1p(jnp.exp(-jnp.abs(z)))


def _fox_proj_kernel(x_ref, g_ref, w_ref, wf_ref, bf_ref, q_ref, kvb_ref, kv_ref, lf_ref):
    h = _rms(x_ref[...], g_ref[...]).astype(BF16)
    q_ref[...] = _dot(h, w_ref[:, :FW]).astype(BF16)
    kv = _dot(h, w_ref[:, FW:])
    kv_ref[...] = kv
    kvb_ref[...] = kv.astype(BF16)
    lf_ref[...] = _log_sigmoid(_dot(h, wf_ref[...]) + bf_ref[...])


def fox_project(x, g_mix, w_qkv, w_f, b_f):
    n, d = x.shape
    tm = _row_tile(n, 256)
    row = lambda wd: pl.BlockSpec((tm, wd), lambda i: (i, 0))
    full = lambda a: pl.BlockSpec(a.shape, lambda i: (0,) * a.ndim)
    return pl.pallas_call(
        _fox_proj_kernel,
        grid=(n // tm,),
        in_specs=[row(d), full(g_mix), full(w_qkv), full(w_f), full(b_f)],
        out_specs=[row(FW), row(2 * FW), row(2 * FW), row(LANES)],
        out_shape=[jax.ShapeDtypeStruct((n, FW), BF16),
                   jax.ShapeDtypeStruct((n, 2 * FW), BF16),
                   jax.ShapeDtypeStruct((n, 2 * FW), F32),
                   jax.ShapeDtypeStruct((n, LANES), F32)],
        compiler_params=_params("parallel"),
        name="fox_project",
    )(x, g_mix, w_qkv, w_f, b_f)


def _cumsum_kernel(lf_ref, tri_ref, c_ref):
    tri = tri_ref[...]
    carry = jnp.zeros((1, LANES), F32)
    for c in range(lf_ref.shape[1] // PAGE):
        hi, mid, lo = _split3(lf_ref[0, c * PAGE:(c + 1) * PAGE, :])
        y = _dot(tri, hi) + _dot(tri, mid) + _dot(tri, lo) + carry
        c_ref[0, c * PAGE:(c + 1) * PAGE, :] = y
        carry = y[PAGE - 1:PAGE, :]


def cumsum_rows(lf, batch, seq):
    tri = jnp.tril(jnp.ones((PAGE, PAGE), F32)).astype(BF16)
    return pl.pallas_call(
        _cumsum_kernel,
        grid=(batch,),
        in_specs=[pl.BlockSpec((1, seq, LANES), lambda b: (b, 0, 0)),
                  pl.BlockSpec((PAGE, PAGE), lambda b: (0, 0))],
        out_specs=pl.BlockSpec((1, seq, LANES), lambda b: (b, 0, 0)),
        out_shape=jax.ShapeDtypeStruct((batch, seq, LANES), F32),
        compiler_params=_params("parallel"),
        name="cumsum_rows",
    )(lf.reshape(batch, seq, LANES), tri)


def _fox_cum_kernel(pt_ref, *refs, pp):
    page_refs = refs[:pp]
    lfn_ref, triu_ref, ck_ref, cn_ref, carry_sc = refs[pp:]
    j = pl.program_id(1)
    triu = triu_ref[...]

    def lane_cumsum(x):
        hi, mid, lo = _split3(x)
        return _dot(hi, triu) + _dot(mid, triu) + _dot(lo, triu)

    @pl.when(j == 0)
    def _():
        carry_sc[...] = jnp.zeros(carry_sc.shape, F32)

    x = jnp.concatenate([r[0, 0] for r in page_refs], axis=0)
    w = lane_cumsum(x)
    carry = carry_sc[...]
    for t in range(pp):
        wt = w[t * FOX_HEADS:(t + 1) * FOX_HEADS, :]
        ck_ref[0, t] = wt + carry
        carry = carry + wt[:, PAGE - 1:PAGE]
    carry_sc[...] = carry

    @pl.when(j == pl.num_programs(1) - 1)
    def _():
        cn_ref[0] = lane_cumsum(lfn_ref[0]) + carry


def fox_past_cumsum(logf_t, layer, page_table, lf_new_t, pp):
    bd, n_pages = page_table.shape
    triu = jnp.triu(jnp.ones((PAGE, PAGE), F32)).astype(BF16)
    in_specs = ([pl.BlockSpec((1, 1, FOX_HEADS, PAGE), functools.partial(
                    lambda b, j, pt, t: (layer, pt[b, j * pp + t], 0, 0), t=t)) for t in range(pp)]
                + [pl.BlockSpec((1, FOX_HEADS, PAGE), lambda b, j, pt: (b, 0, 0)),
                   pl.BlockSpec((PAGE, PAGE), lambda b, j, pt: (0, 0))])
    grid_spec = pltpu.PrefetchScalarGridSpec(
        num_scalar_prefetch=1,
        grid=(bd, n_pages // pp),
        in_specs=in_specs,
        out_specs=[pl.BlockSpec((1, pp, FOX_HEADS, PAGE), lambda b, j, pt: (b, j, 0, 0)),
                   pl.BlockSpec((1, FOX_HEADS, PAGE), lambda b, j, pt: (b, 0, 0))],
        scratch_shapes=[pltpu.VMEM((FOX_HEADS, 1), F32)],
    )
    return pl.pallas_call(
        functools.partial(_fox_cum_kernel, pp=pp),
        grid_spec=grid_spec,
        out_shape=[jax.ShapeDtypeStruct((bd, n_pages, FOX_HEADS, PAGE), F32),
                   jax.ShapeDtypeStruct((bd, FOX_HEADS, PAGE), F32)],
        compiler_params=_params("parallel", "arbitrary"),
        name="fox_past_cumsum",
    )(page_table, *([logf_t] * pp), lf_new_t, triu)


def _expand_rows(c, nq):
    h = c.shape[0]
    return jnp.broadcast_to(c[:, None, :], (h, nq, c.shape[1])).reshape(h * nq, c.shape[1])


def _fox_sample_kernel(pt_ref, q_ref, *rest, pp, nq):
    kv_refs = rest[:pp]
    ck_ref, ccol_ref, cnt_ref, new_ref, o_ref, m_sc, l_sc, acc_sc, kn_sc = rest[pp:]
    j = pl.program_id(1)
    q = q_ref[0]
    rows = q.shape[0]
    ccol = ccol_ref[0]

    @pl.when(j == 0)
    def _():
        _softmax_init(m_sc, l_sc, acc_sc)
        kn_sc[...] = jnp.zeros(kn_sc.shape, F32)
        kn_sc[0:nq, :] = new_ref[0]
        kn = kn_sc[...]
        vals = kn[:, FW:].astype(BF16)
        s = _dot_nt(q, kn[:, :FW].astype(BF16)) + (ccol - _expand_rows(cnt_ref[0], nq))
        s = jnp.where(_new_key_mask(rows, nq), s, NEG_INF)
        _online_update(s, lambda p: _dot(p, vals), m_sc, l_sc, acc_sc)

    kt = jnp.concatenate([r[0, 0, 0] for r in kv_refs], axis=1).astype(BF16)
    vt = jnp.concatenate([r[0, 0, 1] for r in kv_refs], axis=1).astype(BF16)
    ck = jnp.concatenate([ck_ref[0, t] for t in range(pp)], axis=1)
    s = _dot(q, kt) + (ccol - _expand_rows(ck, nq))
    _online_update(s, lambda p: _dot_nt(p, vt), m_sc, l_sc, acc_sc)

    @pl.when(j == pl.num_programs(1) - 1)
    def _():
        o_bd = acc_sc[...] / l_sc[...]
        lane = lax.broadcasted_iota(jnp.int32, (nq, LANES), 1)

        def store(pr, val):
            o_ref[0, :, pr * LANES:(pr + 1) * LANES] = val

        _diag_pairs(o_bd, lane, nq, store)


def fox_sample_attention(q_bd, cache_kv_t, layer, page_table, c_keys, c_col, c_new_t, kv_new, pp):
    bd, rows, _ = q_bd.shape
    n_pages = page_table.shape[1]
    nq = kv_new.shape[1]
    in_specs = ([pl.BlockSpec((1, rows, FW), lambda b, j, pt: (b, 0, 0))]
                + [pl.BlockSpec((1, 1, 2, FW, PAGE), functools.partial(
                    lambda b, j, pt, t: (layer, pt[b, j * pp + t], 0, 0, 0), t=t)) for t in range(pp)]
                + [pl.BlockSpec((1, pp, FOX_HEADS, PAGE), lambda b, j, pt: (b, j, 0, 0)),
                   pl.BlockSpec((1, rows, 1), lambda b, j, pt: (b, 0, 0)),
                   pl.BlockSpec((1, FOX_HEADS, PAGE), lambda b, j, pt: (b, 0, 0)),
                   pl.BlockSpec((1, nq, 2 * FW), lambda b, j, pt: (b, 0, 0))])
    grid_spec = pltpu.PrefetchScalarGridSpec(
        num_scalar_prefetch=1,
        grid=(bd, n_pages // pp),
        in_specs=in_specs,
        out_specs=pl.BlockSpec((1, nq, FW), lambda b, j, pt: (b, 0, 0)),
        scratch_shapes=[pltpu.VMEM((rows, 1), F32), pltpu.VMEM((rows, 1), F32),
                        pltpu.VMEM((rows, FW), F32), pltpu.VMEM((PAGE, 2 * FW), F32)],
    )
    return pl.pallas_call(
        functools.partial(_fox_sample_kernel, pp=pp, nq=nq),
        grid_spec=grid_spec,
        out_shape=jax.ShapeDtypeStruct((bd, nq, FW), F32),
        compiler_params=_params("parallel", "arbitrary"),
        name="fox_sample_attention",
    )(page_table, q_bd, *([cache_kv_t] * pp), c_keys, c_col, c_new_t, kv_new)


def _mla_weights(w_dq, g_q, w_uq, w_dkv, g_kv, w_ukv, w_o):
    pad = LANES - MLA_NOPE - MLA_ROPE
    uq = w_uq.reshape(MLA_Q_RANK, MLA_HEADS, MLA_NOPE + MLA_ROPE)
    uq = jnp.pad(uq, ((0, 0), (0, 0), (0, pad))).reshape(MLA_Q_RANK, MLA_HEADS * LANES)
    ukv = w_ukv.reshape(MLA_KV_RANK, MLA_HEADS, MLA_NOPE + MLA_V)
    uk, uv = ukv[..., :MLA_NOPE], ukv[..., MLA_NOPE:]
    uk_pad = jnp.pad(uk, ((0, 0), (0, 0), (0, LANES - MLA_NOPE))).reshape(MLA_KV_RANK, MLA_HEADS * LANES)
    wdr = jnp.pad(w_dkv[:, MLA_KV_RANK:], ((0, 0), (ROPE_LO, pad)))
    w_abs = jnp.zeros((MLA_HEADS, LANES, QW), F32)
    w_abs = w_abs.at[:, :MLA_NOPE, :MLA_KV_RANK].set(uk.transpose(1, 2, 0))
    w_abs = w_abs.at[:, ROPE_LO:ROPE_LO + MLA_ROPE, MLA_KV_RANK:MLA_KV_RANK + MLA_ROPE].set(
        jnp.broadcast_to(jnp.eye(MLA_ROPE, dtype=F32), (MLA_HEADS, MLA_ROPE, MLA_ROPE)))
    uvh = uv.transpose(1, 0, 2).reshape(MLA_HEADS // 2, 2, MLA_KV_RANK, MLA_V)
    w_un = jnp.zeros((MLA_HEADS // 2, 2, MLA_KV_RANK, LANES), F32)
    w_un = w_un.at[:, 0, :, :MLA_V].set(uvh[:, 0]).at[:, 1, :, MLA_V:].set(uvh[:, 1])
    return {
        "wdq": w_dq.astype(BF16), "gq": g_q.reshape(1, -1), "wuq": uq.astype(BF16),
        "wdc": w_dkv[:, :MLA_KV_RANK].astype(BF16), "wdr": wdr.astype(BF16), "gkv": g_kv.reshape(1, -1),
        "wuk": uk_pad.astype(BF16), "wuv": uv.reshape(MLA_KV_RANK, MLA_HEADS * MLA_V).astype(BF16),
        "wabs": w_abs.astype(BF16), "wun": w_un.astype(BF16), "wo": w_o.astype(BF16),
    }


def kernel(x_prompt, x_sample, page_table, cache_mla_ckv, cache_mla_kpe, state_dsw_kv_g0, state_dsw_kv_g1, state_dsw_kv_g2, cache_fox_kv, cache_fox_logf, rel_bias, g_mix, g_ffn, g_final, w_ffn_gate, w_ffn_up, w_ffn_down, a_w_dq, a_g_q, a_w_uq, a_w_dkv, a_g_kv, a_w_ukv, a_w_o, b_w_qkv, b_w_o, c_w_qkv, c_w_f, c_b_f, c_w_o):
    batch, seq, d = x_prompt.shape
    bd, nq, _ = x_sample.shape
    n_pages = page_table.shape[1]
    past = n_pages * PAGE
    depth = g_mix.shape[0]
    n_p, n_s = batch * seq, bd * nq
    page_table = page_table.astype(jnp.int32)
    dsw_states = (state_dsw_kv_g0, state_dsw_kv_g1, state_dsw_kv_g2)

    xp = x_prompt.reshape(n_p, d)
    xs = x_sample.reshape(n_s, d)

    rope_p = _rope_tables(jnp.arange(seq))
    rope_s = _rope_tables(jnp.tile(past + jnp.arange(nq), bd))

    fox_logf_t = cache_fox_logf.transpose(0, 1, 3, 2)
    fox_kv_t = cache_fox_kv.transpose(0, 1, 3, 4, 5, 2).reshape(cache_fox_kv.shape[:2] + (2, FW, PAGE))
    mla_kpe_t = cache_mla_kpe.transpose(0, 1, 3, 2)

    mla_p, mla_s, dsw_p, dsw_s, fox_p, fox_s = [], [], [], [], [], []
    for i in range(depth):
        kind, j = i % 3, i // 3
        gm = g_mix[i].reshape(1, d)
        if kind == 0:
            w = _mla_weights(a_w_dq[j], a_g_q[j], a_w_uq[j], a_w_dkv[j], a_g_kv[j], a_w_ukv[j], a_w_o[j])
            q, kf, v, ckv, kpe = mla_project(xp, gm, w, rope_p, seq)
            o = flash_prompt(q, kf, v, batch, seq, fox=False)
            xp = matmul_residual(o, w["wo"], xp)
            mla_p.append((ckv.reshape(batch, seq, MLA_KV_RANK),
                          kpe[:, ROPE_LO:ROPE_LO + MLA_ROPE].reshape(batch, seq, MLA_ROPE)))
            q, _, _, ckv, kpe = mla_project(xs, gm, w, rope_s, n_s)
            ckv_new = ckv.reshape(bd, nq, MLA_KV_RANK)
            kpe_new = kpe[:, ROPE_LO:ROPE_LO + MLA_ROPE].reshape(bd, nq, MLA_ROPE)
            qabs = mla_absorb(q, w["wabs"], bd, nq).reshape(bd, MLA_HEADS * nq, QW)
            o_lat = mla_sample_attention(qabs, cache_mla_ckv, mla_kpe_t, j, page_table,
                                         ckv_new, kpe_new, pp=16 if n_pages % 16 == 0 else 1, eb=1)
            o = mla_unabsorb(o_lat.reshape(bd, MLA_HEADS, nq, MLA_KV_RANK), w["wun"], bd, nq)
            xs = matmul_residual(o, w["wo"], xs)
            mla_s.append((ckv_new, kpe_new))
        elif kind == 1:
            qscale = jnp.tile(jnp.concatenate([jnp.full((GW,), DSW_SCALE, F32), jnp.ones((2 * GW,), F32)]), N_GROUPS)
            w_qkv = (b_w_qkv[j] * qscale[None, :]).astype(BF16)
            w_o = b_w_o[j].astype(BF16)
            *qkv_sub, kv32 = dsw_project_prompt(xp, gm, w_qkv, batch, seq)
            outs, lses, st = [], [], []
            for g, (win, dil) in enumerate(DSW_GROUPS):
                o_g, lse_g = dsw_prompt_group(qkv_sub[g], g, dil, _dsw_prompt_bias(rel_bias, g, dil), batch, seq)
                outs.append(o_g)
                lses.append(lse_g)
                kv_g = kv32[:, g * 2 * GW:(g + 1) * 2 * GW].reshape(batch, seq, 2, DSW_HEADS, DSW_HEAD_DIM)
                st.append(kv_g[:, seq - min(win, seq):])
            xp = dsw_merge_out(outs, lses, w_o, xp)
            dsw_p.append(st)
            qkv, kv32 = dsw_project(xs, gm, w_qkv)
            outs, lses, st = [], [], []
            for g, (win, dil) in enumerate(DSW_GROUPS):
                buf = dsw_states[g][j]
                wb = buf.shape[1]
                buf_t = buf.transpose(0, 2, 3, 4, 1).reshape(bd, 2, GW, wb)
                qg = qkv[:, g * 3 * GW:g * 3 * GW + GW].reshape(bd * nq, DSW_SPLIT, HW)
                q_bd = jnp.stack([_block_diag_q(qg[:, s], bd, nq, DSW_HEADS // DSW_SPLIT, DSW_HEAD_DIM)
                                  for s in range(DSW_SPLIT)], axis=1)
                kv_new = kv32[:, g * 2 * GW:(g + 1) * 2 * GW].reshape(bd, nq, 2, GW)
                new_t = jnp.pad(kv_new.transpose(0, 2, 3, 1), ((0, 0), (0, 0), (0, 0), (0, LANES - nq)))
                bias, bias_new = _dsw_sample_bias(rel_bias, g, dil, wb, nq)
                rows = DSW_HEADS * nq // DSW_SPLIT
                nbuf_t, o_g, lse_g = dsw_sample_group(buf_t, new_t, q_bd, bias.reshape(DSW_SPLIT, rows, wb),
                                                      bias_new.reshape(DSW_SPLIT, rows, PAGE), nq)
                outs.append(o_g.reshape(n_s, GW))
                lses.append(lse_g.reshape(n_s, GW))
                st.append(nbuf_t.reshape(bd, 2, DSW_HEADS, DSW_HEAD_DIM, wb).transpose(0, 4, 1, 2, 3))
            xs = dsw_merge_out(outs, lses, w_o, xs)
            dsw_s.append(st)
        else:
            qscale = jnp.concatenate([jnp.full((FW,), FOX_SCALE, F32), jnp.ones((2 * FW,), F32)])
            w_qkv = (c_w_qkv[j] * qscale[None, :]).astype(BF16)
            w_f = jnp.pad(c_w_f[j], ((0, 0), (0, LANES - FOX_HEADS))).astype(BF16)
            b_f = jnp.pad(c_b_f[j], (0, LANES - FOX_HEADS)).reshape(1, LANES)
            w_o = c_w_o[j].astype(BF16)
            q, kvb, kv32, lf = fox_project(xp, gm, w_qkv, w_f, b_f)
            c = cumsum_rows(lf, batch, seq)[:, :, :FOX_HEADS]
            ccol = c.reshape(batch, seq, FOX_HEADS // 2, 2).transpose(0, 2, 1, 3)
            crow = c.transpose(0, 2, 1).reshape(batch, FOX_HEADS // 2, 2, seq)
            o = flash_prompt(q, kvb[:, :FW], kvb[:, FW:], batch, seq, fox=True, ccol=ccol, crow=crow)
            xp = matmul_residual(o, w_o, xp)
            fox_p.append((kv32.reshape(batch, seq, 2, FOX_HEADS, FOX_HEAD_DIM),
                          lf[:, :FOX_HEADS].reshape(batch, seq, FOX_HEADS)))
            q, _, kv32, lf = fox_project(xs, gm, w_qkv, w_f, b_f)
            lf_new = lf[:, :FOX_HEADS].reshape(bd, nq, FOX_HEADS)
            lf_new_t = jnp.pad(lf_new.transpose(0, 2, 1), ((0, 0), (0, 0), (0, PAGE - nq)))
            c_keys, c_new_t = fox_past_cumsum(fox_logf_t, j, page_table, lf_new_t,
                                              pp=16 if n_pages % 16 == 0 else 1)
            c_col = c_new_t[:, :, :nq].reshape(bd, FOX_HEADS * nq, 1)
            q_bd = _block_diag_q(q, bd, nq, FOX_HEADS, FOX_HEAD_DIM)
            o = fox_sample_attention(q_bd, fox_kv_t, j, page_table, c_keys, c_col, c_new_t,
                                     kv32.reshape(bd, nq, 2 * FW), pp=8 if n_pages % 8 == 0 else 1)
            xs = matmul_residual(o.reshape(n_s, FW), w_o, xs)
            fox_s.append((kv32.reshape(bd, nq, 2, FOX_HEADS, FOX_HEAD_DIM), lf_new))
        wg, wu, wd = w_ffn_gate[i].astype(BF16), w_ffn_up[i].astype(BF16), w_ffn_down[i].astype(BF16)
        xp = ffn_block(xp, g_ffn[i], wg, wu, wd)
        xs = ffn_block(xs, g_ffn[i], wg, wu, wd)

    y_prompt = final_norm(xp, g_final).reshape(batch, seq, d)
    y_sample = final_norm(xs, g_final).reshape(bd, nq, d)
    stack = lambda rows, k: jnp.stack([r[k] for r in rows])
    return (y_prompt, y_sample,
            stack(mla_p, 0), stack(mla_p, 1), stack(mla_s, 0), stack(mla_s, 1),
            stack(dsw_p, 0), stack(dsw_p, 1), stack(dsw_p, 2),
            stack(dsw_s, 0), stack(dsw_s, 1), stack(dsw_s, 2),
            stack(fox_p, 0), stack(fox_p, 1), stack(fox_s, 0), stack(fox_s, 1))
```
